```python
import jax, jax.numpy as jnp
from jax import lax
import numpy as np

D_MODEL = 1024
BATCH = 2
SEQ = 16384
DEPTH = 1
DEC_BATCH = 8
DEC_SEQ = 4096
PAST_LEN = 128

GRID_W = 64
D_CONV = 512
CONV_K = 31
N_HEADS = 8
HEAD_DIM = 64
D_ATTN = N_HEADS * HEAD_DIM
WIN_ROWS = 8
WIN_COLS = 16
Q_COL_BLOCK = 16
K_COL_BLOCK = 32
D_IN = 2 * D_CONV + 3 * D_ATTN
PEER_HEADS = 8
PEER_KEYS = 128
PEER_EXPERTS = PEER_KEYS * PEER_KEYS
PEER_DK_HALF = 128
PEER_TOPK = 16
PEER_CHUNK = 128
EPS = 1e-6
NEG = -1e30

kernel_name = "hybrid_conv_natten_peer_encoder"


def _rmsnorm(x, g):
    x32 = x.astype(jnp.float32)
    y = x32 * lax.rsqrt(jnp.mean(x32 * x32, axis=-1, keepdims=True) + EPS)
    return (y * g.astype(jnp.float32)).astype(x.dtype)


def _layernorm(x, g, b):
    x32 = x.astype(jnp.float32)
    mu = jnp.mean(x32, axis=-1, keepdims=True)
    xc = x32 - mu
    y = xc * lax.rsqrt(jnp.mean(xc * xc, axis=-1, keepdims=True) + EPS)
    return (y * g.astype(jnp.float32) + b.astype(jnp.float32)).astype(x.dtype)


def _col_tables():
    n_cb = GRID_W // Q_COL_BLOCK
    qcol = np.arange(n_cb)[:, None] * Q_COL_BLOCK + np.arange(Q_COL_BLOCK)[None, :]
    kstart = np.clip(np.arange(n_cb) * Q_COL_BLOCK - WIN_COLS // 2, 0, GRID_W - K_COL_BLOCK)
    kcol = kstart[:, None] + np.arange(K_COL_BLOCK)[None, :]
    cs = np.clip(qcol - WIN_COLS // 2, 0, GRID_W - WIN_COLS)[..., None]
    kc = kcol[:, None, :]
    mask = (kc >= cs) & (kc < cs + WIN_COLS)
    off_idx = np.clip(kc - qcol[..., None] + WIN_COLS - 1, 0, 2 * WIN_COLS - 2)
    return kcol, mask, off_idx


def _conformer_conv(a, gate, conv_w, conv_b, ln_g, ln_b):
    u = a * jax.nn.sigmoid(gate)
    y = lax.conv_general_dilated(
        u, conv_w[:, None, :].astype(u.dtype), window_strides=(1,),
        padding=[(CONV_K // 2, CONV_K // 2)],
        dimension_numbers=("NWC", "WIO", "NWC"),
        feature_group_count=D_CONV) + conv_b.astype(u.dtype)
    y = _layernorm(y, ln_g, ln_b)
    return jax.nn.silu(y)


def _neighbourhood_attention(q, k, v, rpb):
    b, t = q.shape[0], q.shape[1]
    rows = t // GRID_W
    wr = min(WIN_ROWS, rows)
    n_cb = GRID_W // Q_COL_BLOCK
    kcol, mask, off_idx = _col_tables()
    mask = jnp.asarray(mask)[:, :, None, :]
    q = q.reshape(b, rows, GRID_W, N_HEADS, HEAD_DIM)
    k = k.reshape(b, rows, GRID_W, N_HEADS, HEAD_DIM)
    v = v.reshape(b, rows, GRID_W, N_HEADS, HEAD_DIM)
    scale = HEAD_DIM ** -0.5

    def row_step(r):
        rs = jnp.clip(r - wr // 2, 0, rows - wr)
        qr = lax.dynamic_index_in_dim(q, r, axis=1, keepdims=False)
        qr = qr.reshape(b, n_cb, Q_COL_BLOCK, N_HEADS, HEAD_DIM)
        kb = lax.dynamic_slice_in_dim(k, rs, wr, axis=1)[:, :, kcol]
        vb = lax.dynamic_slice_in_dim(v, rs, wr, axis=1)[:, :, kcol]
        roff_idx = rs + jnp.arange(wr) - r + WIN_ROWS - 1
        bias = rpb[:, roff_idx][:, :, off_idx]
        bias = jnp.transpose(bias, (0, 2, 3, 1, 4)).astype(jnp.float32)
        s = jnp.einsum("bjqhd,brjkhd->bhjqrk", qr, kb).astype(jnp.float32) * scale + bias
        s = jnp.where(mask, s, NEG)
        p = jax.nn.softmax(s.reshape(b, N_HEADS, n_cb, Q_COL_BLOCK, wr * K_COL_BLOCK), axis=-1)
        p = p.reshape(s.shape).astype(v.dtype)
        o = jnp.einsum("bhjqrk,brjkhd->bjqhd", p, vb)
        return o.reshape(b, GRID_W, D_ATTN)

    out = lax.map(row_step, jnp.arange(rows))
    return jnp.transpose(out, (1, 0, 2, 3)).reshape(b, t, D_ATTN)


def _peer(h, w_query, sub_keys, expert_u, expert_v):
    b, t, d = h.shape
    xs = h.reshape(-1, PEER_CHUNK, d)

    def chunk(xc):
        qry = (xc @ w_query).reshape(PEER_CHUNK, PEER_HEADS, 2, PEER_DK_HALF)
        s = jnp.einsum("chpd,hpkd->chpk", qry, sub_keys).astype(jnp.float32)
        sv, si = lax.top_k(s, PEER_TOPK)
        cand = (sv[:, :, 0, :, None] + sv[:, :, 1, None, :]).reshape(
            PEER_CHUNK, PEER_HEADS, PEER_TOPK * PEER_TOPK)
        cv, ci = lax.top_k(cand, PEER_TOPK)
        i1 = jnp.take_along_axis(si[:, :, 0], ci // PEER_TOPK, axis=-1)
        i2 = jnp.take_along_axis(si[:, :, 1], ci % PEER_TOPK, axis=-1)
        e = i1 * PEER_KEYS + i2
        g = jax.nn.softmax(cv, axis=-1)
        u = expert_u[e]
        a = jax.nn.gelu(jnp.einsum("cd,chkd->chk", xc, u).astype(jnp.float32), approximate=False)
        return jnp.einsum("chk,chkd->cd", (g * a).astype(xc.dtype), expert_v[e])

    return lax.map(chunk, xs).reshape(b, t, d)


def _layer(x, g_mix, w_in, conv_w, conv_b, conv_ln_g, conv_ln_b, q_norm_g, k_norm_g, rpb,
           g_out_conv, g_out_attn, w_out, g_ffn, w_query, sub_keys, expert_u, expert_v):
    b, t, _ = x.shape
    h = _rmsnorm(x, g_mix)
    proj = h @ w_in
    a, gate, q, k, v = jnp.split(
        proj, [D_CONV, 2 * D_CONV, 2 * D_CONV + D_ATTN, 2 * D_CONV + 2 * D_ATTN], axis=-1)
    y_conv = _conformer_conv(a, gate, conv_w, conv_b, conv_ln_g, conv_ln_b)
    q = _rmsnorm(q.reshape(b, t, N_HEADS, HEAD_DIM), q_norm_g)
    k = _rmsnorm(k.reshape(b, t, N_HEADS, HEAD_DIM), k_norm_g)
    v = v.reshape(b, t, N_HEADS, HEAD_DIM)
    y_attn = _neighbourhood_attention(q, k, v, rpb)
    mixed = jnp.concatenate([_rmsnorm(y_conv, g_out_conv), _rmsnorm(y_attn, g_out_attn)], axis=-1)
    x = x + mixed @ w_out
    x = x + _peer(_rmsnorm(x, g_ffn), w_query, sub_keys, expert_u, expert_v)
    return x


def setup_inputs(seed: int = 0) -> dict:
    key = jax.random.key(seed)
    ks = jax.random.split(key, 20)
    f32 = jnp.float32
    L = DEPTH

    def nrm(k, shape, std):
        return jax.random.normal(k, shape, f32) * std

    def gain(k, shape):
        return 1.0 + 0.02 * jax.random.normal(k, shape, f32)

    return {
        "x_prompt": jax.random.normal(ks[0], (BATCH, SEQ, D_MODEL), f32),
        "x_sample": jax.random.normal(ks[1], (DEC_BATCH, DEC_SEQ, D_MODEL), f32),
        "g_mix": gain(ks[2], (L, D_MODEL)),
        "w_in": nrm(ks[3], (L, D_MODEL, D_IN), D_MODEL ** -0.5),
        "conv_w": nrm(ks[4], (L, CONV_K, D_CONV), CONV_K ** -0.5),
        "conv_b": nrm(ks[5], (L, D_CONV), 0.02),
        "conv_ln_g": gain(ks[6], (L, D_CONV)),
        "conv_ln_b": nrm(ks[7], (L, D_CONV), 0.02),
        "q_norm_g": gain(ks[8], (L, HEAD_DIM)),
        "k_norm_g": gain(ks[9], (L, HEAD_DIM)),
        "rpb": nrm(ks[10], (L, N_HEADS, 2 * WIN_ROWS - 1, 2 * WIN_COLS - 1), 0.1),
        "g_out_conv": gain(ks[11], (L, D_CONV)),
        "g_out_attn": gain(ks[12], (L, D_ATTN)),
        "w_out": nrm(ks[13], (L, D_CONV + D_ATTN, D_MODEL), (D_CONV + D_ATTN) ** -0.5),
        "g_ffn": gain(ks[14], (L, D_MODEL)),
        "w_query": nrm(ks[15], (L, D_MODEL, PEER_HEADS * 2 * PEER_DK_HALF), D_MODEL ** -0.5),
        "sub_keys": nrm(ks[16], (L, PEER_HEADS, 2, PEER_KEYS, PEER_DK_HALF), PEER_DK_HALF ** -0.5),
        "expert_u": nrm(ks[17], (L, PEER_EXPERTS, D_MODEL), D_MODEL ** -0.5),
        "expert_v": nrm(ks[18], (L, PEER_EXPERTS, D_MODEL), 0.25),
    }


def reference(x_prompt, x_sample, g_mix, w_in, conv_w, conv_b, conv_ln_g, conv_ln_b,
              q_norm_g, k_norm_g, rpb, g_out_conv, g_out_attn, w_out, g_ffn,
              w_query, sub_keys, expert_u, expert_v):
    y_prompt = x_prompt
    y_sample = x_sample
    for l in range(DEPTH):
        p = (g_mix[l], w_in[l], conv_w[l], conv_b[l], conv_ln_g[l], conv_ln_b[l],
             q_norm_g[l], k_norm_g[l], rpb[l], g_out_conv[l], g_out_attn[l], w_out[l],
             g_ffn[l], w_query[l], sub_keys[l], expert_u[l], expert_v[l])
        y_prompt = _layer(y_prompt, *p)
        y_sample = _layer(y_sample, *p)
    return (y_prompt, y_sample)
```

```python
import functools

import numpy as np
import jax
import jax.numpy as jnp
from jax import lax
from jax.experimental import pallas as pl
from jax.experimental.pallas import tpu as pltpu

EPS = 1e-6
NEG = -1e30

D_MODEL = 1024
D_CONV = 512
D_ATTN = 512
N_HEADS = 8
HEAD_DIM = 64
CONV_K = 31
GRID_W = 64
WIN_ROWS = 8
WIN_COLS = 16

PEER_HEADS = 8
PEER_KEYS = 128
PEER_DK_HALF = 128
PEER_TOPK = 16
N_SEL = PEER_HEADS * PEER_TOPK
N_EXPERTS = PEER_KEYS * PEER_KEYS

LANES = 128
ROW_WORDS = D_MODEL // 2
ROW_PLANES = ROW_WORDS // LANES
PLANE_ROWS = N_SEL + 8

Q_ROWS = 4
HALO = 16

VMEM_LIMIT = 56 * 1024 * 1024


def _params(n_axes=1):
    return pltpu.CompilerParams(
        dimension_semantics=("arbitrary",) * n_axes, vmem_limit_bytes=VMEM_LIMIT)


def _rms(x, g):
    return x * lax.rsqrt(jnp.mean(x * x, axis=-1, keepdims=True) + EPS) * g


def _dot_nt(a, b):
    return lax.dot_general(a, b, (((1,), (1,)), ((), ())), preferred_element_type=jnp.float32)


def _in_proj_kernel(x_ref, g_ref, w_ref, hsum_ref, qg_ref, kg_ref, u_ref, q_ref, k_ref, v_ref):
    h = _rms(x_ref[...], g_ref[...])
    proj = jnp.dot(h.astype(jnp.bfloat16), w_ref[...], preferred_element_type=jnp.float32)
    a = proj[:, 0:D_CONV]
    gate = proj[:, D_CONV:2 * D_CONV]
    u_ref[...] = a * jax.nn.sigmoid(gate)

    def head_rms(t, gain):
        tt = t * t
        hi = tt.astype(jnp.bfloat16)
        lo = (tt - hi.astype(jnp.float32)).astype(jnp.bfloat16)
        ms = (jnp.dot(hi, hsum_ref[...], preferred_element_type=jnp.float32)
              + jnp.dot(lo, hsum_ref[...], preferred_element_type=jnp.float32))
        return t * lax.rsqrt(ms + EPS) * gain

    o = 2 * D_CONV
    q_ref[...] = head_rms(proj[:, o:o + D_ATTN], qg_ref[...]).astype(jnp.bfloat16)
    k_ref[...] = head_rms(proj[:, o + D_ATTN:o + 2 * D_ATTN], kg_ref[...]).astype(jnp.bfloat16)
    v_ref[...] = proj[:, o + 2 * D_ATTN:o + 3 * D_ATTN].astype(jnp.bfloat16)


def _in_proj(x, g_mix, w_in, hsum, qg, kg, tm=512):
    n = x.shape[0]
    d_in = w_in.shape[1]
    row = lambda i: (i, 0)
    fix = lambda i: (0, 0)
    return pl.pallas_call(
        _in_proj_kernel,
        grid=(n // tm,),
        in_specs=[
            pl.BlockSpec((tm, D_MODEL), row),
            pl.BlockSpec((1, D_MODEL), fix),
            pl.BlockSpec((D_MODEL, d_in), fix),
            pl.BlockSpec((D_ATTN, D_ATTN), fix),
            pl.BlockSpec((1, D_ATTN), fix),
            pl.BlockSpec((1, D_ATTN), fix),
        ],
        out_specs=[pl.BlockSpec((tm, D_CONV), row)] + [pl.BlockSpec((tm, D_ATTN), row)] * 3,
        out_shape=[jax.ShapeDtypeStruct((n, D_CONV), jnp.float32)]
        + [jax.ShapeDtypeStruct((n, D_ATTN), jnp.bfloat16)] * 3,
        compiler_params=_params(),
        name="in_proj",
    )(x, g_mix, w_in, hsum, qg, kg)


def _conv_kernel(prev_ref, cur_ref, next_ref, w_ref, b_ref, lg_ref, lb_ref, og_ref, out_ref, xp_ref,
                 *, chunk):
    t = pl.program_id(1)
    nt = pl.num_programs(1)
    tt = cur_ref.shape[0]
    xp_ref[0:HALO, :] = jnp.where(t > 0, prev_ref[...], 0.0)
    xp_ref[HALO:HALO + tt, :] = cur_ref[...]
    xp_ref[HALO + tt:HALO + tt + HALO, :] = jnp.where(t < nt - 1, next_ref[...], 0.0)

    def body(i, carry):
        base = pl.multiple_of(i * chunk, chunk)
        win = xp_ref[pl.ds(base, chunk + 2 * HALO), :]
        acc = jnp.zeros((chunk, D_CONV), jnp.float32)
        for j in range(CONV_K):
            o = HALO - CONV_K // 2 + j
            acc = acc + win[o:o + chunk, :] * w_ref[j:j + 1, :]
        y = acc + b_ref[...]
        mu = jnp.mean(y, axis=-1, keepdims=True)
        yc = y - mu
        yn = yc * lax.rsqrt(jnp.mean(yc * yc, axis=-1, keepdims=True) + EPS) * lg_ref[...] + lb_ref[...]
        z = yn * jax.nn.sigmoid(yn)
        out_ref[pl.ds(base, chunk), :] = _rms(z, og_ref[...]).astype(out_ref.dtype)
        return carry

    lax.fori_loop(0, tt // chunk, body, 0)


def _conv(u, conv_w, conv_b, ln_g, ln_b, og, tt=512, chunk=32):
    b, t, _ = u.shape
    hb = tt // HALO
    n_halo = t // HALO
    fix = lambda bi, ti: (0, 0)
    return pl.pallas_call(
        functools.partial(_conv_kernel, chunk=chunk),
        grid=(b, t // tt),
        in_specs=[
            pl.BlockSpec((None, HALO, D_CONV), lambda bi, ti: (bi, jnp.maximum(ti * hb - 1, 0), 0)),
            pl.BlockSpec((None, tt, D_CONV), lambda bi, ti: (bi, ti, 0)),
            pl.BlockSpec((None, HALO, D_CONV),
                         lambda bi, ti: (bi, jnp.minimum((ti + 1) * hb, n_halo - 1), 0)),
            pl.BlockSpec((CONV_K, D_CONV), fix),
            pl.BlockSpec((1, D_CONV), fix),
            pl.BlockSpec((1, D_CONV), fix),
            pl.BlockSpec((1, D_CONV), fix),
            pl.BlockSpec((1, D_CONV), fix),
        ],
        out_specs=pl.BlockSpec((None, tt, D_CONV), lambda bi, ti: (bi, ti, 0)),
        out_shape=jax.ShapeDtypeStruct((b, t, D_CONV), jnp.bfloat16),
        scratch_shapes=[pltpu.VMEM((tt + 2 * HALO, D_CONV), jnp.float32)],
        compiler_params=_params(2),
        name="conv",
    )(u, u, u, conv_w, conv_b, ln_g, ln_b, og)


def _natten_kernel(q_ref, kp_ref, kc_ref, kn_ref, vp_ref, vc_ref, vn_ref, bias_ref, og_ref, out_ref):
    lane = lax.broadcasted_iota(jnp.int32, (1, LANES), 1)
    first_head = lane < HEAD_DIM
    outs = []
    for p in range(N_HEADS // 2):
        sl = slice(p * LANES, (p + 1) * LANES)
        q2 = q_ref[:, sl]
        kcat = jnp.concatenate([kp_ref[:, sl], kc_ref[:, sl], kn_ref[:, sl]], axis=0)
        vcat = jnp.concatenate([vp_ref[:, sl], vc_ref[:, sl], vn_ref[:, sl]], axis=0)
        o_pair = []
        for sub in range(2):
            mine = first_head if sub == 0 else jnp.logical_not(first_head)
            qm = jnp.where(mine, q2, jnp.zeros_like(q2))
            s = _dot_nt(qm, kcat) + bias_ref[2 * p + sub]
            m = jnp.max(s, axis=-1, keepdims=True)
            e = jnp.exp(s - m)
            l = jnp.sum(e, axis=-1, keepdims=True)
            o = jnp.dot(e.astype(jnp.bfloat16), vcat, preferred_element_type=jnp.float32)
            o_pair.append(o / l)
        outs.append(jnp.where(first_head, o_pair[0], o_pair[1]))
    y = jnp.concatenate(outs, axis=1)
    out_ref[...] = _rms(y, og_ref[...]).astype(out_ref.dtype)


def _natten(q, k, v, bias, og):
    b, t, _ = q.shape
    tq = Q_ROWS * GRID_W
    nb = t // tq
    prev = lambda bi, i: (bi, jnp.maximum(i - 1, 0), 0)
    cur = lambda bi, i: (bi, i, 0)
    nxt = lambda bi, i: (bi, jnp.minimum(i + 1, nb - 1), 0)
    blk = lambda m: pl.BlockSpec((None, tq, D_ATTN), m)
    variant = lambda bi, i: (jnp.where(i == 0, 0, jnp.where(i == nb - 1, 2, 1)), 0, 0, 0)
    return pl.pallas_call(
        _natten_kernel,
        grid=(b, nb),
        in_specs=[
            blk(cur), blk(prev), blk(cur), blk(nxt), blk(prev), blk(cur), blk(nxt),
            pl.BlockSpec((None, N_HEADS, tq, 3 * tq), variant),
            pl.BlockSpec((1, D_ATTN), lambda bi, i: (0, 0)),
        ],
        out_specs=blk(cur),
        out_shape=jax.ShapeDtypeStruct((b, t, D_ATTN), jnp.bfloat16),
        compiler_params=_params(2),
        name="natten",
    )(q, k, k, k, v, v, v, bias, og)


def _natten_bias(rpb, rows):
    tq = Q_ROWS * GRID_W
    a = (np.arange(tq) // GRID_W)[:, None]
    c = (np.arange(tq) % GRID_W)[:, None]
    i = (np.arange(3 * tq) // GRID_W)[None, :]
    kc = (np.arange(3 * tq) % GRID_W)[None, :]
    cs = np.clip(c - WIN_COLS // 2, 0, GRID_W - WIN_COLS)
    col_ok = (kc >= cs) & (kc < cs + WIN_COLS)
    dc = np.clip(kc - c + WIN_COLS - 1, 0, 2 * WIN_COLS - 2)
    dr = np.broadcast_to(i - Q_ROWS - a + WIN_ROWS - 1, (tq, 3 * tq))
    tabs = []
    for r0, clamp in ((0, True), (0, False), (rows - Q_ROWS, True)):
        r = r0 + a
        kr = r0 - Q_ROWS + i
        rs = r - WIN_ROWS // 2
        if clamp:
            rs = np.clip(rs, 0, rows - WIN_ROWS)
        ok = col_ok & (kr >= rs) & (kr < rs + WIN_ROWS)
        tab = rpb[:, dr, np.broadcast_to(dc, dr.shape)].astype(jnp.float32)
        tabs.append(jnp.where(jnp.asarray(ok)[None], tab, NEG))
    return jnp.stack(tabs, axis=0)


def _out_proj_kernel(x_ref, yc_ref, ya_ref, wo_ref, g_ref, wq_ref, sk_ref, x1_ref, h2_ref, st_ref):
    mix = (jnp.dot(yc_ref[...], wo_ref[0:D_CONV, :], preferred_element_type=jnp.float32)
           + jnp.dot(ya_ref[...], wo_ref[D_CONV:D_CONV + D_ATTN, :], preferred_element_type=jnp.float32))
    x1 = x_ref[...] + mix
    x1_ref[...] = x1
    h2 = _rms(x1, g_ref[...])
    h2_ref[...] = h2
    qry = jnp.dot(h2.astype(jnp.bfloat16), wq_ref[...], preferred_element_type=jnp.float32)
    qry = qry.astype(jnp.bfloat16)
    for j in range(2 * PEER_HEADS):
        st_ref[j] = _dot_nt(sk_ref[j], qry[:, j * PEER_DK_HALF:(j + 1) * PEER_DK_HALF])


def _out_proj(x, yc, ya, w_out, g_ffn, w_query, sub_keys, tm=256):
    n = x.shape[0]
    row = lambda i: (i, 0)
    fix = lambda i: (0, 0)
    nq = w_query.shape[1]
    return pl.pallas_call(
        _out_proj_kernel,
        grid=(n // tm,),
        in_specs=[
            pl.BlockSpec((tm, D_MODEL), row),
            pl.BlockSpec((tm, D_CONV), row),
            pl.BlockSpec((tm, D_ATTN), row),
            pl.BlockSpec((D_CONV + D_ATTN, D_MODEL), fix),
            pl.BlockSpec((1, D_MODEL), fix),
            pl.BlockSpec((D_MODEL, nq), fix),
            pl.BlockSpec((2 * PEER_HEADS, PEER_KEYS, PEER_DK_HALF), lambda i: (0, 0, 0)),
        ],
        out_specs=[
            pl.BlockSpec((tm, D_MODEL), row),
            pl.BlockSpec((tm, D_MODEL), row),
            pl.BlockSpec((2 * PEER_HEADS, PEER_KEYS, tm), lambda i: (0, 0, i)),
        ],
        out_shape=[
            jax.ShapeDtypeStruct((n, D_MODEL), jnp.float32),
            jax.ShapeDtypeStruct((n, D_MODEL), jnp.float32),
            jax.ShapeDtypeStruct((2 * PEER_HEADS, PEER_KEYS, n), jnp.float32),
        ],
        compiler_params=_params(),
        name="out_proj",
    )(x, yc, ya, w_out, g_ffn, w_query, sub_keys)


def _top_rows(s, row_id, count):
    big = jnp.int32(2 ** 30)
    vals, ids = [], []
    for _ in range(count):
        m = jnp.max(s, axis=0, keepdims=True)
        sel = jnp.min(jnp.where(s == m, row_id, big), axis=0, keepdims=True)
        vals.append(m)
        ids.append(sel)
        s = jnp.where(row_id == sel, -jnp.inf, s)
    return vals, ids


def _route_kernel(st_ref, idx_ref, gate_ref):
    tn = st_ref.shape[2]
    key_id = lax.broadcasted_iota(jnp.int32, (PEER_KEYS, tn), 0)
    sub8 = lax.broadcasted_iota(jnp.int32, (8, tn), 0)
    sub16 = lax.broadcasted_iota(jnp.int32, (PEER_TOPK, tn), 0)
    big = jnp.int32(2 ** 30)
    e_all, g_all = [], []
    for h in range(PEER_HEADS):
        sv0, si0 = _top_rows(st_ref[2 * h], key_id, PEER_TOPK)
        sv1, si1 = _top_rows(st_ref[2 * h + 1], key_id, PEER_TOPK)
        v1 = jnp.concatenate(sv1, axis=0)
        i1 = jnp.concatenate(si1, axis=0)
        cand, eid, flat = [], [], []
        for i in range(PEER_TOPK // 2):
            nj = PEER_TOPK // (i + 1)
            rows = PEER_TOPK if nj > 8 else 8
            sub = sub16 if rows == PEER_TOPK else sub8
            c = sv0[i] + v1[0:rows]
            if nj < rows:
                c = jnp.where(sub < nj, c, -jnp.inf)
            cand.append(c)
            eid.append(si0[i] * PEER_KEYS + i1[0:rows])
            flat.append(i * PEER_TOPK + sub)
        v0_tail = jnp.concatenate(sv0[PEER_TOPK // 2:], axis=0)
        i0_tail = jnp.concatenate(si0[PEER_TOPK // 2:], axis=0)
        cand.append(v0_tail + sv1[0])
        eid.append(i0_tail * PEER_KEYS + si1[0])
        flat.append((sub8 + PEER_TOPK // 2) * PEER_TOPK)
        cand = jnp.concatenate(cand, axis=0)
        eid = jnp.concatenate(eid, axis=0)
        flat = jnp.concatenate(flat, axis=0)
        cvs, es = [], []
        for _ in range(PEER_TOPK):
            m = jnp.max(cand, axis=0, keepdims=True)
            fsel = jnp.min(jnp.where(cand == m, flat, big), axis=0, keepdims=True)
            hit = flat == fsel
            es.append(jnp.sum(jnp.where(hit, eid, 0), axis=0, keepdims=True))
            cvs.append(m)
            cand = jnp.where(hit, -jnp.inf, cand)
        cv = jnp.concatenate(cvs, axis=0)
        ex = jnp.exp(cv - cvs[0])
        g_all.append(ex / jnp.sum(ex, axis=0, keepdims=True))
        e_all.append(jnp.concatenate(es, axis=0))
    e_all = jnp.concatenate(e_all, axis=0)
    g_all = jnp.concatenate(g_all, axis=0)
    idx_ref[...] = pltpu.bitcast(pltpu.bitcast(e_all, jnp.float32).T, jnp.int32)
    gate_ref[...] = g_all.T


def _route(st, tn=256):
    n = st.shape[2]
    return pl.pallas_call(
        _route_kernel,
        grid=(n // tn,),
        in_specs=[pl.BlockSpec((2 * PEER_HEADS, PEER_KEYS, tn), lambda i: (0, 0, i))],
        out_specs=[pl.BlockSpec((tn, N_SEL), lambda i: (i, 0))] * 2,
        out_shape=[jax.ShapeDtypeStruct((n, N_SEL), jnp.int32),
                   jax.ShapeDtypeStruct((n, N_SEL), jnp.float32)],
        compiler_params=_params(),
        name="route",
    )(st)


def _load_table(tab_hbm, tab_vmem, sem):
    @pl.when(pl.program_id(0) == 0)
    def _():
        cp = pltpu.make_async_copy(tab_hbm, tab_vmem, sem)
        cp.start()
        cp.wait()


def _gather_rows(idx_row, tab_ref, stage_ref):
    for k in range(N_SEL):
        stage_ref[pl.ds(k, ROW_PLANES, stride=PLANE_ROWS), :] = tab_ref[idx_row[k]]


def _staged_matrix(stage_ref):
    los, his = [], []
    for s in range(ROW_PLANES):
        p = stage_ref[pl.ds(PLANE_ROWS * s, N_SEL), :]
        los.append(pltpu.bitcast(p << 16, jnp.float32))
        his.append(pltpu.bitcast(p & jnp.int32(-65536), jnp.float32))
    return jnp.concatenate(los + his, axis=1).astype(jnp.bfloat16)


def _split_rows(row):
    hi = row.astype(jnp.bfloat16)
    lo = (row - hi.astype(jnp.float32)).astype(jnp.bfloat16)
    return jnp.concatenate([hi, lo, jnp.zeros((6, row.shape[1]), jnp.bfloat16)], axis=0)


def _peer_u_kernel(idx_ref, h_ref, gate_ref, ut_hbm, w_ref, ut_vmem, stage_ref, sem):
    _load_table(ut_hbm, ut_vmem, sem)

    def tok(c, carry):
        _gather_rows(idx_ref.at[c], ut_vmem, stage_ref)
        r = _staged_matrix(stage_ref)
        a8 = _dot_nt(_split_rows(h_ref[pl.ds(c, 1), :]), r)
        a = a8[0:1] + a8[1:2]
        act = 0.5 * a * (1.0 + lax.erf(a * np.float32(1.0 / np.sqrt(2.0))))
        w_ref[pl.ds(c, 1), :] = gate_ref[pl.ds(c, 1), :] * act
        return carry

    lax.fori_loop(0, h_ref.shape[0], tok, 0)


def _peer_v_kernel(idx_ref, w_ref, x_ref, vt_hbm, out_ref, vt_vmem, stage_ref, sem):
    _load_table(vt_hbm, vt_vmem, sem)

    def tok(c, carry):
        _gather_rows(idx_ref.at[c], vt_vmem, stage_ref)
        r = _staged_matrix(stage_ref)
        o8 = jnp.dot(_split_rows(w_ref[pl.ds(c, 1), :]), r, preferred_element_type=jnp.float32)
        out_ref[pl.ds(c, 1), :] = x_ref[pl.ds(c, 1), :] + (o8[0:1] + o8[1:2])
        return carry

    lax.fori_loop(0, x_ref.shape[0], tok, 0)


def _peer_call(kernel_fn, name, idx, a, b, table, out_cols, tm=256):
    n = idx.shape[0]
    row = lambda i: (i, 0)
    return pl.pallas_call(
        kernel_fn,
        grid=(n // tm,),
        in_specs=[
            pl.BlockSpec((tm, N_SEL), row, memory_space=pltpu.SMEM),
            pl.BlockSpec((tm, a.shape[1]), row),
            pl.BlockSpec((tm, b.shape[1]), row),
            pl.BlockSpec(memory_space=pl.ANY),
        ],
        out_specs=pl.BlockSpec((tm, out_cols), row),
        out_shape=jax.ShapeDtypeStruct((n, out_cols), jnp.float32),
        scratch_shapes=[pltpu.VMEM((N_EXPERTS, ROW_PLANES, LANES), jnp.int32),
                        pltpu.VMEM((ROW_PLANES * PLANE_ROWS, LANES), jnp.int32),
                        pltpu.SemaphoreType.DMA],
        compiler_params=_params(),
        name=name,
    )(idx, a, b, table)


def _pack_table(t):
    tb = t.astype(jnp.bfloat16)
    lo = lax.bitcast_convert_type(tb[:, :ROW_WORDS], jnp.uint16).astype(jnp.uint32)
    hi = lax.bitcast_convert_type(tb[:, ROW_WORDS:], jnp.uint16).astype(jnp.uint32)
    words = lax.bitcast_convert_type(lo | (hi << 16), jnp.int32)
    return words.reshape(t.shape[0], ROW_PLANES, LANES)


def _layer(x, p):
    b, t, _ = x.shape
    n = b * t
    xf = x.reshape(n, D_MODEL)
    u, q, k, v = _in_proj(xf, p["g_mix"], p["w_in"], p["hsum"], p["qg"], p["kg"])
    seq = lambda z: z.reshape(b, t, z.shape[-1])
    yc = _conv(seq(u), p["conv_w"], p["conv_b"], p["conv_ln_g"], p["conv_ln_b"], p["g_out_conv"])
    bias = _natten_bias(p["rpb"], t // GRID_W)
    ya = _natten(seq(q), seq(k), seq(v), bias, p["g_out_attn"])
    x1, h2, st = _out_proj(xf, yc.reshape(n, D_CONV), ya.reshape(n, D_ATTN), p["w_out"], p["g_ffn"],
                           p["w_query"], p["sub_keys"])
    idx, gate = _route(st)
    w = _peer_call(_peer_u_kernel, "peer_u", idx, h2, gate, p["ut"], N_SEL)
    y = _peer_call(_peer_v_kernel, "peer_v", idx, w, x1, p["vt"], D_MODEL)
    return y.reshape(b, t, D_MODEL)


def kernel(x_prompt, x_sample, g_mix, w_in, conv_w, conv_b, conv_ln_g, conv_ln_b, q_norm_g, k_norm_g, rpb, g_out_conv, g_out_attn, w_out, g_ffn, w_query, sub_keys, expert_u, expert_v):
    depth = g_mix.shape[0]
    head = np.arange(D_ATTN) // HEAD_DIM
    hsum = jnp.asarray((head[:, None] == head[None, :]) / HEAD_DIM, jnp.bfloat16)
    scale = HEAD_DIM ** -0.5
    y_prompt, y_sample = x_prompt, x_sample
    for l in range(depth):
        row = lambda z: z[l].reshape(1, -1)
        p = dict(
            g_mix=row(g_mix), w_in=w_in[l].astype(jnp.bfloat16), hsum=hsum,
            qg=jnp.tile(q_norm_g[l], N_HEADS).reshape(1, -1) * scale,
            kg=jnp.tile(k_norm_g[l], N_HEADS).reshape(1, -1),
            conv_w=conv_w[l], conv_b=row(conv_b), conv_ln_g=row(conv_ln_g), conv_ln_b=row(conv_ln_b),
            g_out_conv=row(g_out_conv), g_out_attn=row(g_out_attn), rpb=rpb[l],
            w_out=w_out[l].astype(jnp.bfloat16), g_ffn=row(g_ffn),
            w_query=w_query[l].astype(jnp.bfloat16),
            sub_keys=sub_keys[l].reshape(2 * PEER_HEADS, PEER_KEYS, PEER_DK_HALF).astype(jnp.bfloat16),
            ut=_pack_table(expert_u[l]), vt=_pack_table(expert_v[l]),
        )
        y_prompt = _layer(y_prompt, p)
        y_sample = _layer(y_sample, p)
    return (y_prompt, y_sample)
```

```python
import functools

import numpy as np
import jax
import jax.numpy as jnp
from jax import lax
from jax.experimental import pallas as pl
from jax.experimental.pallas import tpu as pltpu

EPS = 1e-6
NEG = -1e30

D_MODEL = 1024
D_CONV = 512
D_ATTN = 512
N_HEADS = 8
HEAD_DIM = 64
CONV_K = 31
GRID_W = 64
WIN_ROWS = 8
WIN_COLS = 16

PEER_HEADS = 8
PEER_KEYS = 128
PEER_DK_HALF = 128
PEER_TOPK = 16
N_SEL = PEER_HEADS * PEER_TOPK
N_EXPERTS = PEER_KEYS * PEER_KEYS

LANES = 128
ROW_WORDS = D_MODEL // 2
ROW_PLANES = ROW_WORDS // LANES
PLANE_ROWS = N_SEL + 8

Q_ROWS = 4
HALO = 16

VMEM_LIMIT = 56 * 1024 * 1024


def _params(n_axes=1, flags=None):
    return pltpu.CompilerParams(
        dimension_semantics=("arbitrary",) * n_axes, vmem_limit_bytes=VMEM_LIMIT, flags=flags)


def _rms(x, g):
    return x * lax.rsqrt(jnp.mean(x * x, axis=-1, keepdims=True) + EPS) * g


def _dot_nt(a, b):
    return lax.dot_general(a, b, (((1,), (1,)), ((), ())), preferred_element_type=jnp.float32)


def _in_proj_kernel(x_ref, g_ref, w_ref, hsum_ref, qg_ref, kg_ref, u_ref, q_ref, k_ref, v_ref):
    h = _rms(x_ref[...], g_ref[...])
    proj = jnp.dot(h.astype(jnp.bfloat16), w_ref[...], preferred_element_type=jnp.float32)
    a = proj[:, 0:D_CONV]
    gate = proj[:, D_CONV:2 * D_CONV]
    u_ref[...] = a * jax.nn.sigmoid(gate)

    def head_rms(t, gain):
        tt = t * t
        hi = tt.astype(jnp.bfloat16)
        lo = (tt - hi.astype(jnp.float32)).astype(jnp.bfloat16)
        ms = (jnp.dot(hi, hsum_ref[...], preferred_element_type=jnp.float32)
              + jnp.dot(lo, hsum_ref[...], preferred_element_type=jnp.float32))
        return t * lax.rsqrt(ms + EPS) * gain

    o = 2 * D_CONV
    q_ref[...] = head_rms(proj[:, o:o + D_ATTN], qg_ref[...]).astype(jnp.bfloat16)
    k_ref[...] = head_rms(proj[:, o + D_ATTN:o + 2 * D_ATTN], kg_ref[...]).astype(jnp.bfloat16)
    v_ref[...] = proj[:, o + 2 * D_ATTN:o + 3 * D_ATTN].astype(jnp.bfloat16)


def _in_proj(x, g_mix, w_in, hsum, qg, kg, tm=512):
    n = x.shape[0]
    d_in = w_in.shape[1]
    row = lambda i: (i, 0)
    fix = lambda i: (0, 0)
    return pl.pallas_call(
        _in_proj_kernel,
        grid=(n // tm,),
        in_specs=[
            pl.BlockSpec((tm, D_MODEL), row),
            pl.BlockSpec((1, D_MODEL), fix),
            pl.BlockSpec((D_MODEL, d_in), fix),
            pl.BlockSpec((D_ATTN, D_ATTN), fix),
            pl.BlockSpec((1, D_ATTN), fix),
            pl.BlockSpec((1, D_ATTN), fix),
        ],
        out_specs=[pl.BlockSpec((tm, D_CONV), row)] + [pl.BlockSpec((tm, D_ATTN), row)] * 3,
        out_shape=[jax.ShapeDtypeStruct((n, D_CONV), jnp.float32)]
        + [jax.ShapeDtypeStruct((n, D_ATTN), jnp.bfloat16)] * 3,
        compiler_params=_params(),
        name="in_proj",
    )(x, g_mix, w_in, hsum, qg, kg)


def _conv_kernel(prev_ref, cur_ref, next_ref, w_ref, b_ref, lg_ref, lb_ref, og_ref, out_ref, xp_ref,
                 *, chunk):
    t = pl.program_id(1)
    nt = pl.num_programs(1)
    tt = cur_ref.shape[0]
    xp_ref[0:HALO, :] = jnp.where(t > 0, prev_ref[...], 0.0)
    xp_ref[HALO:HALO + tt, :] = cur_ref[...]
    xp_ref[HALO + tt:HALO + tt + HALO, :] = jnp.where(t < nt - 1, next_ref[...], 0.0)

    def body(i, carry):
        base = pl.multiple_of(i * chunk, chunk)
        win = xp_ref[pl.ds(base, chunk + 2 * HALO), :]
        acc = jnp.zeros((chunk, D_CONV), jnp.float32)
        for j in range(CONV_K):
            o = HALO - CONV_K // 2 + j
            acc = acc + win[o:o + chunk, :] * w_ref[j:j + 1, :]
        y = acc + b_ref[...]
        mu = jnp.mean(y, axis=-1, keepdims=True)
        yc = y - mu
        yn = yc * lax.rsqrt(jnp.mean(yc * yc, axis=-1, keepdims=True) + EPS) * lg_ref[...] + lb_ref[...]
        z = yn * jax.nn.sigmoid(yn)
        out_ref[pl.ds(base, chunk), :] = _rms(z, og_ref[...]).astype(out_ref.dtype)
        return carry

    lax.fori_loop(0, tt // chunk, body, 0)


def _conv(u, conv_w, conv_b, ln_g, ln_b, og, tt=512, chunk=32):
    b, t, _ = u.shape
    hb = tt // HALO
    n_halo = t // HALO
    fix = lambda bi, ti: (0, 0)
    return pl.pallas_call(
        functools.partial(_conv_kernel, chunk=chunk),
        grid=(b, t // tt),
        in_specs=[
            pl.BlockSpec((None, HALO, D_CONV), lambda bi, ti: (bi, jnp.maximum(ti * hb - 1, 0), 0)),
            pl.BlockSpec((None, tt, D_CONV), lambda bi, ti: (bi, ti, 0)),
            pl.BlockSpec((None, HALO, D_CONV),
                         lambda bi, ti: (bi, jnp.minimum((ti + 1) * hb, n_halo - 1), 0)),
            pl.BlockSpec((CONV_K, D_CONV), fix),
            pl.BlockSpec((1, D_CONV), fix),
            pl.BlockSpec((1, D_CONV), fix),
            pl.BlockSpec((1, D_CONV), fix),
            pl.BlockSpec((1, D_CONV), fix),
        ],
        out_specs=pl.BlockSpec((None, tt, D_CONV), lambda bi, ti: (bi, ti, 0)),
        out_shape=jax.ShapeDtypeStruct((b, t, D_CONV), jnp.bfloat16),
        scratch_shapes=[pltpu.VMEM((tt + 2 * HALO, D_CONV), jnp.float32)],
        compiler_params=_params(2),
        name="conv",
    )(u, u, u, conv_w, conv_b, ln_g, ln_b, og)


def _natten_kernel(q_ref, kp_ref, kc_ref, kn_ref, vp_ref, vc_ref, vn_ref, bias_ref, og_ref, out_ref):
    lane = lax.broadcasted_iota(jnp.int32, (1, LANES), 1)
    first_head = lane < HEAD_DIM
    outs = []
    for p in range(N_HEADS // 2):
        sl = slice(p * LANES, (p + 1) * LANES)
        q2 = q_ref[:, sl]
        kcat = jnp.concatenate([kp_ref[:, sl], kc_ref[:, sl], kn_ref[:, sl]], axis=0)
        vcat = jnp.concatenate([vp_ref[:, sl], vc_ref[:, sl], vn_ref[:, sl]], axis=0)
        o_pair = []
        for sub in range(2):
            mine = first_head if sub == 0 else jnp.logical_not(first_head)
            qm = jnp.where(mine, q2, jnp.zeros_like(q2))
            s = _dot_nt(qm, kcat) + bias_ref[2 * p + sub]
            m = jnp.max(s, axis=-1, keepdims=True)
            e = jnp.exp(s - m)
            l = jnp.sum(e, axis=-1, keepdims=True)
            o = jnp.dot(e.astype(jnp.bfloat16), vcat, preferred_element_type=jnp.float32)
            o_pair.append(o / l)
        outs.append(jnp.where(first_head, o_pair[0], o_pair[1]))
    y = jnp.concatenate(outs, axis=1)
    out_ref[...] = _rms(y, og_ref[...]).astype(out_ref.dtype)


def _natten(q, k, v, bias, og):
    b, t, _ = q.shape
    tq = Q_ROWS * GRID_W
    nb = t // tq
    prev = lambda bi, i: (bi, jnp.maximum(i - 1, 0), 0)
    cur = lambda bi, i: (bi, i, 0)
    nxt = lambda bi, i: (bi, jnp.minimum(i + 1, nb - 1), 0)
    blk = lambda m: pl.BlockSpec((None, tq, D_ATTN), m)
    variant = lambda bi, i: (jnp.where(i == 0, 0, jnp.where(i == nb - 1, 2, 1)), 0, 0, 0)
    return pl.pallas_call(
        _natten_kernel,
        grid=(b, nb),
        in_specs=[
            blk(cur), blk(prev), blk(cur), blk(nxt), blk(prev), blk(cur), blk(nxt),
            pl.BlockSpec((None, N_HEADS, tq, 3 * tq), variant),
            pl.BlockSpec((1, D_ATTN), lambda bi, i: (0, 0)),
        ],
        out_specs=blk(cur),
        out_shape=jax.ShapeDtypeStruct((b, t, D_ATTN), jnp.bfloat16),
        compiler_params=_params(2),
        name="natten",
    )(q, k, k, k, v, v, v, bias, og)


def _natten_bias(rpb, rows):
    tq = Q_ROWS * GRID_W
    nh, n_dr, n_dc = rpb.shape
    span = 2 * GRID_W - 1
    lo = (GRID_W - 1) - (WIN_COLS - 1)
    padded = jnp.pad(rpb.astype(jnp.float32), ((0, 0), (0, 0), (lo, span - n_dc - lo)))
    flat = jnp.tile(padded, (1, 1, GRID_W))[:, :, GRID_W - 1:GRID_W - 1 + GRID_W * (span - 1)]
    col_t = flat.reshape(nh, n_dr, GRID_W, span - 1)[..., :GRID_W]
    per_a = [col_t[:, WIN_ROWS - 1 - Q_ROWS - a:WIN_ROWS - 1 - Q_ROWS - a + 3 * Q_ROWS]
             for a in range(Q_ROWS)]
    tab = jnp.stack(per_a, axis=1)
    tab = tab.transpose(0, 1, 3, 2, 4).reshape(nh, tq, 3 * tq)

    a = (np.arange(tq) // GRID_W)[:, None]
    c = (np.arange(tq) % GRID_W)[:, None]
    i = (np.arange(3 * tq) // GRID_W)[None, :]
    kc = (np.arange(3 * tq) % GRID_W)[None, :]
    cs = np.clip(c - WIN_COLS // 2, 0, GRID_W - WIN_COLS)
    col_ok = (kc >= cs) & (kc < cs + WIN_COLS)
    tabs = []
    for r0, clamp in ((0, True), (0, False), (rows - Q_ROWS, True)):
        r = r0 + a
        kr = r0 - Q_ROWS + i
        rs = r - WIN_ROWS // 2
        if clamp:
            rs = np.clip(rs, 0, rows - WIN_ROWS)
        ok = col_ok & (kr >= rs) & (kr < rs + WIN_ROWS)
        tabs.append(jnp.where(jnp.asarray(ok)[None], tab, NEG))
    return jnp.stack(tabs, axis=0)


def _out_proj_kernel(x_ref, yc_ref, ya_ref, wo_ref, g_ref, wq_ref, sk_ref, x1_ref, h2_ref, st_ref):
    mix = (jnp.dot(yc_ref[...], wo_ref[0:D_CONV, :], preferred_element_type=jnp.float32)
           + jnp.dot(ya_ref[...], wo_ref[D_CONV:D_CONV + D_ATTN, :], preferred_element_type=jnp.float32))
    x1 = x_ref[...] + mix
    x1_ref[...] = x1
    h2 = _rms(x1, g_ref[...])
    h2_ref[...] = h2
    qry = jnp.dot(h2.astype(jnp.bfloat16), wq_ref[...], preferred_element_type=jnp.float32)
    qry = qry.astype(jnp.bfloat16)
    for j in range(2 * PEER_HEADS):
        st_ref[j] = _dot_nt(sk_ref[j], qry[:, j * PEER_DK_HALF:(j + 1) * PEER_DK_HALF])


def _out_proj(x, yc, ya, w_out, g_ffn, w_query, sub_keys, tm=256):
    n = x.shape[0]
    row = lambda i: (i, 0)
    fix = lambda i: (0, 0)
    nq = w_query.shape[1]
    return pl.pallas_call(
        _out_proj_kernel,
        grid=(n // tm,),
        in_specs=[
            pl.BlockSpec((tm, D_MODEL), row),
            pl.BlockSpec((tm, D_CONV), row),
            pl.BlockSpec((tm, D_ATTN), row),
            pl.BlockSpec((D_CONV + D_ATTN, D_MODEL), fix),
            pl.BlockSpec((1, D_MODEL), fix),
            pl.BlockSpec((D_MODEL, nq), fix),
            pl.BlockSpec((2 * PEER_HEADS, PEER_KEYS, PEER_DK_HALF), lambda i: (0, 0, 0)),
        ],
        out_specs=[
            pl.BlockSpec((tm, D_MODEL), row),
            pl.BlockSpec((tm, D_MODEL), row),
            pl.BlockSpec((2 * PEER_HEADS, PEER_KEYS, tm), lambda i: (0, 0, i)),
        ],
        out_shape=[
            jax.ShapeDtypeStruct((n, D_MODEL), jnp.float32),
            jax.ShapeDtypeStruct((n, D_MODEL), jnp.float32),
            jax.ShapeDtypeStruct((2 * PEER_HEADS, PEER_KEYS, n), jnp.float32),
        ],
        compiler_params=_params(),
        name="out_proj",
    )(x, yc, ya, w_out, g_ffn, w_query, sub_keys)


def _top_rows(s, row_id, count):
    big = jnp.int32(2 ** 30)
    vals, ids = [], []
    for _ in range(count):
        m = jnp.max(s, axis=0, keepdims=True)
        sel = jnp.min(jnp.where(s == m, row_id, big), axis=0, keepdims=True)
        vals.append(m)
        ids.append(sel)
        s = jnp.where(row_id == sel, -jnp.inf, s)
    return vals, ids


def _route_kernel(st_ref, idx_ref, gate_ref):
    tn = st_ref.shape[2]
    key_id = lax.broadcasted_iota(jnp.int32, (PEER_KEYS, tn), 0)
    sub8 = lax.broadcasted_iota(jnp.int32, (8, tn), 0)
    sub16 = lax.broadcasted_iota(jnp.int32, (PEER_TOPK, tn), 0)
    big = jnp.int32(2 ** 30)
    e_all, g_all = [], []
    for h in range(PEER_HEADS):
        sv0, si0 = _top_rows(st_ref[2 * h], key_id, PEER_TOPK)
        sv1, si1 = _top_rows(st_ref[2 * h + 1], key_id, PEER_TOPK)
        v1 = jnp.concatenate(sv1, axis=0)
        i1 = jnp.concatenate(si1, axis=0)
        cand, eid, flat = [], [], []
        for i in range(PEER_TOPK // 2):
            nj = PEER_TOPK // (i + 1)
            rows = PEER_TOPK if nj > 8 else 8
            sub = sub16 if rows == PEER_TOPK else sub8
            c = sv0[i] + v1[0:rows]
            if nj < rows:
                c = jnp.where(sub < nj, c, -jnp.inf)
            cand.append(c)
            eid.append(si0[i] * PEER_KEYS + i1[0:rows])
            flat.append(i * PEER_TOPK + sub)
        v0_tail = jnp.concatenate(sv0[PEER_TOPK // 2:], axis=0)
        i0_tail = jnp.concatenate(si0[PEER_TOPK // 2:], axis=0)
        cand.append(v0_tail + sv1[0])
        eid.append(i0_tail * PEER_KEYS + si1[0])
        flat.append((sub8 + PEER_TOPK // 2) * PEER_TOPK)
        cand = jnp.concatenate(cand, axis=0)
        eid = jnp.concatenate(eid, axis=0)
        flat = jnp.concatenate(flat, axis=0)
        cvs, es = [], []
        for _ in range(PEER_TOPK):
            m = jnp.max(cand, axis=0, keepdims=True)
            fsel = jnp.min(jnp.where(cand == m, flat, big), axis=0, keepdims=True)
            hit = flat == fsel
            es.append(jnp.sum(jnp.where(hit, eid, 0), axis=0, keepdims=True))
            cvs.append(m)
            cand = jnp.where(hit, -jnp.inf, cand)
        cv = jnp.concatenate(cvs, axis=0)
        ex = jnp.exp(cv - cvs[0])
        g_all.append(ex / jnp.sum(ex, axis=0, keepdims=True))
        e_all.append(jnp.concatenate(es, axis=0))
    e_all = jnp.concatenate(e_all, axis=0) * ROW_PLANES
    g_all = jnp.concatenate(g_all, axis=0)
    idx_ref[...] = pltpu.bitcast(pltpu.bitcast(e_all, jnp.float32).T, jnp.int32)
    gate_ref[...] = g_all.T


def _route(st, tn=256):
    n = st.shape[2]
    return pl.pallas_call(
        _route_kernel,
        grid=(n // tn,),
        in_specs=[pl.BlockSpec((2 * PEER_HEADS, PEER_KEYS, tn), lambda i: (0, 0, i))],
        out_specs=[pl.BlockSpec((tn, N_SEL), lambda i: (i, 0))] * 2,
        out_shape=[jax.ShapeDtypeStruct((n, N_SEL), jnp.int32),
                   jax.ShapeDtypeStruct((n, N_SEL), jnp.float32)],
        compiler_params=_params(),
        name="route",
    )(st)


def _load_table(tab_hbm, tab_vmem, sem):
    @pl.when(pl.program_id(0) == 0)
    def _():
        cp = pltpu.make_async_copy(tab_hbm, tab_vmem, sem)
        cp.start()
        cp.wait()


N_OFF = 8


def _gather_rows(row_ref, tab_ref, stage_ref, offs):
    for j in range(N_SEL // N_OFF):
        sub = row_ref.at[pl.ds(j * N_OFF, N_OFF)]
        for i in range(N_OFF):
            k = j * N_OFF + i
            r = pl.multiple_of(sub[offs[i]], ROW_PLANES)
            stage_ref[pl.ds(k, ROW_PLANES, stride=PLANE_ROWS), :] = tab_ref[pl.ds(r, ROW_PLANES), :]


def _staged_matrix(stage_ref):
    los, his = [], []
    for s in range(ROW_PLANES):
        p = stage_ref[pl.ds(PLANE_ROWS * s, N_SEL), :]
        los.append(pltpu.bitcast(p << 16, jnp.float32))
        his.append(pltpu.bitcast(p & jnp.int32(-65536), jnp.float32))
    return jnp.concatenate(los + his, axis=1).astype(jnp.bfloat16)


def _split_rows(row):
    hi = row.astype(jnp.bfloat16)
    lo = (row - hi.astype(jnp.float32)).astype(jnp.bfloat16)
    return jnp.concatenate([hi, lo, jnp.zeros((6, row.shape[1]), jnp.bfloat16)], axis=0)


def _token_pipeline(n_tok, idx_ref, tab_ref, stages, issue, finish, placeholder):
    zero = jnp.minimum(pl.program_id(0), 0)
    offs = [zero + i for i in range(N_OFF)]
    for s in range(2):
        _gather_rows(idx_ref.at[s], tab_ref, stages[s], offs)

    def pair(i, prev):
        c = 2 * i
        for s in range(2):
            finish(jnp.maximum(c + s - 2, 0), s, prev[s])
        cur = tuple(issue(c + s, s) for s in range(2))
        for s in range(2):
            _gather_rows(idx_ref.at[jnp.minimum(c + s + 2, n_tok - 1)], tab_ref, stages[s], offs)
        return cur

    last = lax.fori_loop(0, n_tok // 2, pair, (placeholder, placeholder))
    for s in range(2):
        finish(n_tok - 2 + s, s, last[s])


def _peer_u_kernel(idx_ref, h_ref, gate_ref, ut_hbm, w_ref, ut_vmem, stage0, stage1, sem):
    _load_table(ut_hbm, ut_vmem, sem)
    stages = (stage0, stage1)

    def issue(c, slot):
        return _dot_nt(_split_rows(h_ref[pl.ds(c, 1), :]), _staged_matrix(stages[slot]))

    def finish(c, slot, a8):
        a = a8[0:1] + a8[1:2]
        act = 0.5 * a * (1.0 + lax.erf(a * np.float32(1.0 / np.sqrt(2.0))))
        w_ref[pl.ds(c, 1), :] = gate_ref[pl.ds(c, 1), :] * act

    _token_pipeline(h_ref.shape[0], idx_ref, ut_vmem, stages, issue, finish,
                    jnp.zeros((8, N_SEL), jnp.float32))


def _peer_v_kernel(idx_ref, w_ref, x_ref, vt_hbm, out_ref, vt_vmem, stage0, stage1, sem):
    _load_table(vt_hbm, vt_vmem, sem)
    stages = (stage0, stage1)

    def issue(c, slot):
        return jnp.dot(_split_rows(w_ref[pl.ds(c, 1), :]), _staged_matrix(stages[slot]),
                       preferred_element_type=jnp.float32)

    def finish(c, slot, o8):
        out_ref[pl.ds(c, 1), :] = x_ref[pl.ds(c, 1), :] + (o8[0:1] + o8[1:2])

    _token_pipeline(x_ref.shape[0], idx_ref, vt_vmem, stages, issue, finish,
                    jnp.zeros((8, D_MODEL), jnp.float32))


def _peer_call(kernel_fn, name, idx, a, b, table, out_cols, tm=256):
    n = idx.shape[0]
    row = lambda i: (i, 0)
    return pl.pallas_call(
        kernel_fn,
        grid=(n // tm,),
        in_specs=[
            pl.BlockSpec((tm, N_SEL), row, memory_space=pltpu.SMEM),
            pl.BlockSpec((tm, a.shape[1]), row),
            pl.BlockSpec((tm, b.shape[1]), row),
            pl.BlockSpec(memory_space=pl.ANY),
        ],
        out_specs=pl.BlockSpec((tm, out_cols), row),
        out_shape=jax.ShapeDtypeStruct((n, out_cols), jnp.float32),
        scratch_shapes=[pltpu.VMEM((N_EXPERTS * ROW_PLANES, LANES), jnp.int32),
                        pltpu.VMEM((ROW_PLANES * PLANE_ROWS, LANES), jnp.int32),
                        pltpu.VMEM((ROW_PLANES * PLANE_ROWS, LANES), jnp.int32),
                        pltpu.SemaphoreType.DMA],
        compiler_params=_params(),
        name=name,
    )(idx, a, b, table)


def _pack_table(t):
    tb = t.astype(jnp.bfloat16)
    lo = lax.bitcast_convert_type(tb[:, :ROW_WORDS], jnp.uint16).astype(jnp.uint32)
    hi = lax.bitcast_convert_type(tb[:, ROW_WORDS:], jnp.uint16).astype(jnp.uint32)
    words = lax.bitcast_convert_type(lo | (hi << 16), jnp.int32)
    return words.reshape(t.shape[0] * ROW_PLANES, LANES)


def _layer(x, p):
    b, t, _ = x.shape
    n = b * t
    xf = x.reshape(n, D_MODEL)
    u, q, k, v = _in_proj(xf, p["g_mix"], p["w_in"], p["hsum"], p["qg"], p["kg"])
    seq = lambda z: z.reshape(b, t, z.shape[-1])
    yc = _conv(seq(u), p["conv_w"], p["conv_b"], p["conv_ln_g"], p["conv_ln_b"], p["g_out_conv"])
    bias = _natten_bias(p["rpb"], t // GRID_W)
    ya = _natten(seq(q), seq(k), seq(v), bias, p["g_out_attn"])
    x1, h2, st = _out_proj(xf, yc.reshape(n, D_CONV), ya.reshape(n, D_ATTN), p["w_out"], p["g_ffn"],
                           p["w_query"], p["sub_keys"])
    idx, gate = _route(st)
    w = _peer_call(_peer_u_kernel, "peer_u", idx, h2, gate, p["ut"], N_SEL)
    y = _peer_call(_peer_v_kernel, "peer_v", idx, w, x1, p["vt"], D_MODEL)
    return y.reshape(b, t, D_MODEL)


def kernel(x_prompt, x_sample, g_mix, w_in, conv_w, conv_b, conv_ln_g, conv_ln_b, q_norm_g, k_norm_g, rpb, g_out_conv, g_out_attn, w_out, g_ffn, w_query, sub_keys, expert_u, expert_v):
    depth = g_mix.shape[0]
    head = np.arange(D_ATTN) // HEAD_DIM
    hsum = jnp.asarray((head[:, None] == head[None, :]) / HEAD_DIM, jnp.bfloat16)
    scale = HEAD_DIM ** -0.5
    y_prompt, y_sample = x_prompt, x_sample
    for l in range(depth):
        row = lambda z: z[l].reshape(1, -1)
        p = dict(
            g_mix=row(g_mix), w_in=w_in[l].astype(jnp.bfloat16), hsum=hsum,
            qg=jnp.tile(q_norm_g[l], N_HEADS).reshape(1, -1) * scale,
            kg=jnp.tile(k_norm_g[l], N_HEADS).reshape(1, -1),
            conv_w=conv_w[l], conv_b=row(conv_b), conv_ln_g=row(conv_ln_g), conv_ln_b=row(conv_ln_b),
            g_out_conv=row(g_out_conv), g_out_attn=row(g_out_attn), rpb=rpb[l],
            w_out=w_out[l].astype(jnp.bfloat16), g_ffn=row(g_ffn),
            w_query=w_query[l].astype(jnp.bfloat16),
            sub_keys=sub_keys[l].reshape(2 * PEER_HEADS, PEER_KEYS, PEER_DK_HALF).astype(jnp.bfloat16),
            ut=_pack_table(expert_u[l]), vt=_pack_table(expert_v[l]),
        )
        y_prompt = _layer(y_prompt, p)
        y_sample = _layer(y_sample, p)
    return (y_prompt, y_sample)
```

```python
import functools

import numpy as np
import jax
import jax.numpy as jnp
from jax import lax
from jax.experimental import pallas as pl
from jax.experimental.pallas import tpu as pltpu

EPS = 1e-6
NEG = -1e30

D_MODEL = 1024
D_CONV = 512
D_ATTN = 512
N_HEADS = 8
HEAD_DIM = 64
CONV_K = 31
GRID_W = 64
WIN_ROWS = 8
WIN_COLS = 16

PEER_HEADS = 8
PEER_KEYS = 128
PEER_DK_HALF = 128
PEER_TOPK = 16
N_SEL = PEER_HEADS * PEER_TOPK
N_EXPERTS = PEER_KEYS * PEER_KEYS

LANES = 128
SUBLANES = 8
ROW_WORDS = D_MODEL // 2
ROW_PLANES = ROW_WORDS // LANES
PLANE_ROWS = N_SEL + 8

Q_ROWS = 4
HALO = 16

VMEM_LIMIT = 56 * 1024 * 1024


def _params(n_axes=1, flags=None):
    return pltpu.CompilerParams(
        dimension_semantics=("arbitrary",) * n_axes, vmem_limit_bytes=VMEM_LIMIT, flags=flags)


def _rms(x, g):
    return x * lax.rsqrt(jnp.mean(x * x, axis=-1, keepdims=True) + EPS) * g


def _dot_nt(a, b):
    return lax.dot_general(a, b, (((1,), (1,)), ((), ())), preferred_element_type=jnp.float32)


def _in_proj_kernel(x_ref, g_ref, w_ref, hsum_ref, qg_ref, kg_ref, u_ref, q_ref, k_ref, v_ref):
    h = _rms(x_ref[...], g_ref[...])
    proj = jnp.dot(h.astype(jnp.bfloat16), w_ref[...], preferred_element_type=jnp.float32)
    a = proj[:, 0:D_CONV]
    gate = proj[:, D_CONV:2 * D_CONV]
    u_ref[...] = a * jax.nn.sigmoid(gate)

    def head_rms(t, gain):
        tt = t * t
        hi = tt.astype(jnp.bfloat16)
        lo = (tt - hi.astype(jnp.float32)).astype(jnp.bfloat16)
        ms = (jnp.dot(hi, hsum_ref[...], preferred_element_type=jnp.float32)
              + jnp.dot(lo, hsum_ref[...], preferred_element_type=jnp.float32))
        return t * lax.rsqrt(ms + EPS) * gain

    o = 2 * D_CONV
    q_ref[...] = head_rms(proj[:, o:o + D_ATTN], qg_ref[...]).astype(jnp.bfloat16)
    k_ref[...] = head_rms(proj[:, o + D_ATTN:o + 2 * D_ATTN], kg_ref[...]).astype(jnp.bfloat16)
    v_ref[...] = proj[:, o + 2 * D_ATTN:o + 3 * D_ATTN].astype(jnp.bfloat16)


def _in_proj(x, g_mix, w_in, hsum, qg, kg, tm=512):
    n = x.shape[0]
    d_in = w_in.shape[1]
    row = lambda i: (i, 0)
    fix = lambda i: (0, 0)
    return pl.pallas_call(
        _in_proj_kernel,
        grid=(n // tm,),
        in_specs=[
            pl.BlockSpec((tm, D_MODEL), row),
            pl.BlockSpec((1, D_MODEL), fix),
            pl.BlockSpec((D_MODEL, d_in), fix),
            pl.BlockSpec((D_ATTN, D_ATTN), fix),
            pl.BlockSpec((1, D_ATTN), fix),
            pl.BlockSpec((1, D_ATTN), fix),
        ],
        out_specs=[pl.BlockSpec((tm, D_CONV), row)] + [pl.BlockSpec((tm, D_ATTN), row)] * 3,
        out_shape=[jax.ShapeDtypeStruct((n, D_CONV), jnp.float32)]
        + [jax.ShapeDtypeStruct((n, D_ATTN), jnp.bfloat16)] * 3,
        compiler_params=_params(),
        name="in_proj",
    )(x, g_mix, w_in, hsum, qg, kg)


def _conv_kernel(prev_ref, cur_ref, next_ref, w_ref, b_ref, lg_ref, lb_ref, og_ref, out_ref, xp_ref,
                 *shift_refs, chunk):
    t = pl.program_id(1)
    nt = pl.num_programs(1)
    tt = cur_ref.shape[0]
    xp_ref[0:HALO, :] = jnp.where(t > 0, prev_ref[...], 0.0)
    xp_ref[HALO:HALO + tt, :] = cur_ref[...]
    xp_ref[HALO + tt:HALO + tt + HALO, :] = jnp.where(t < nt - 1, next_ref[...], 0.0)

    span = chunk + 2 * HALO - SUBLANES
    first_tap = HALO - CONV_K // 2

    def one_chunk(base, shift_ref):
        win = xp_ref[pl.ds(base, chunk + 2 * HALO), :]
        for r in range(1, SUBLANES):
            shift_ref[r] = win[r:r + span, :]
        acc = jnp.zeros((chunk, D_CONV), jnp.float32)
        for r in range(SUBLANES):
            for a in range(span // SUBLANES):
                j = SUBLANES * a + r - first_tap
                if 0 <= j < CONV_K and SUBLANES * a + chunk <= span:
                    rows = slice(SUBLANES * a, SUBLANES * a + chunk)
                    tap = win[rows, :] if r == 0 else shift_ref[r, rows, :]
                    acc = acc + tap * w_ref[j:j + 1, :]
        y = acc + b_ref[...]
        mu = jnp.mean(y, axis=-1, keepdims=True)
        yc = y - mu
        yn = yc * lax.rsqrt(jnp.mean(yc * yc, axis=-1, keepdims=True) + EPS) * lg_ref[...] + lb_ref[...]
        z = yn * jax.nn.sigmoid(yn)
        out_ref[pl.ds(base, chunk), :] = _rms(z, og_ref[...]).astype(out_ref.dtype)

    def body(i, carry):
        for half, shift_ref in enumerate(shift_refs):
            one_chunk(pl.multiple_of((2 * i + half) * chunk, chunk), shift_ref)
        return carry

    lax.fori_loop(0, tt // (2 * chunk), body, 0)


def _conv(u, conv_w, conv_b, ln_g, ln_b, og, tt=512, chunk=32):
    b, t, _ = u.shape
    hb = tt // HALO
    n_halo = t // HALO
    fix = lambda bi, ti: (0, 0)
    return pl.pallas_call(
        functools.partial(_conv_kernel, chunk=chunk),
        grid=(b, t // tt),
        in_specs=[
            pl.BlockSpec((None, HALO, D_CONV), lambda bi, ti: (bi, jnp.maximum(ti * hb - 1, 0), 0)),
            pl.BlockSpec((None, tt, D_CONV), lambda bi, ti: (bi, ti, 0)),
            pl.BlockSpec((None, HALO, D_CONV),
                         lambda bi, ti: (bi, jnp.minimum((ti + 1) * hb, n_halo - 1), 0)),
            pl.BlockSpec((CONV_K, D_CONV), fix),
            pl.BlockSpec((1, D_CONV), fix),
            pl.BlockSpec((1, D_CONV), fix),
            pl.BlockSpec((1, D_CONV), fix),
            pl.BlockSpec((1, D_CONV), fix),
        ],
        out_specs=pl.BlockSpec((None, tt, D_CONV), lambda bi, ti: (bi, ti, 0)),
        out_shape=jax.ShapeDtypeStruct((b, t, D_CONV), jnp.bfloat16),
        scratch_shapes=[pltpu.VMEM((tt + 2 * HALO, D_CONV), jnp.float32)]
        + [pltpu.VMEM((SUBLANES, chunk + 2 * HALO - SUBLANES, D_CONV), jnp.float32)] * 2,
        compiler_params=_params(2),
        name="conv",
    )(u, u, u, conv_w, conv_b, ln_g, ln_b, og)


def _natten_kernel(q_ref, kp_ref, kc_ref, kn_ref, vp_ref, vc_ref, vn_ref, bias_ref, og_ref, out_ref):
    lane = lax.broadcasted_iota(jnp.int32, (1, LANES), 1)
    first_head = lane < HEAD_DIM
    outs = []
    for p in range(N_HEADS // 2):
        sl = slice(p * LANES, (p + 1) * LANES)
        q2 = q_ref[:, sl]
        kcat = jnp.concatenate([kp_ref[:, sl], kc_ref[:, sl], kn_ref[:, sl]], axis=0)
        vcat = jnp.concatenate([vp_ref[:, sl], vc_ref[:, sl], vn_ref[:, sl]], axis=0)
        o_pair = []
        for sub in range(2):
            mine = first_head if sub == 0 else jnp.logical_not(first_head)
            qm = jnp.where(mine, q2, jnp.zeros_like(q2))
            s = _dot_nt(qm, kcat) + bias_ref[2 * p + sub]
            m = jnp.max(s, axis=-1, keepdims=True)
            e = jnp.exp(s - m)
            l = jnp.sum(e, axis=-1, keepdims=True)
            o = jnp.dot(e.astype(jnp.bfloat16), vcat, preferred_element_type=jnp.float32)
            o_pair.append(o / l)
        outs.append(jnp.where(first_head, o_pair[0], o_pair[1]))
    y = jnp.concatenate(outs, axis=1)
    out_ref[...] = _rms(y, og_ref[...]).astype(out_ref.dtype)


def _natten(q, k, v, bias, og):
    b, t, _ = q.shape
    tq = Q_ROWS * GRID_W
    nb = t // tq
    prev = lambda bi, i: (bi, jnp.maximum(i - 1, 0), 0)
    cur = lambda bi, i: (bi, i, 0)
    nxt = lambda bi, i: (bi, jnp.minimum(i + 1, nb - 1), 0)
    blk = lambda m: pl.BlockSpec((None, tq, D_ATTN), m)
    variant = lambda bi, i: (jnp.where(i == 0, 0, jnp.where(i == nb - 1, 2, 1)), 0, 0, 0)
    return pl.pallas_call(
        _natten_kernel,
        grid=(b, nb),
        in_specs=[
            blk(cur), blk(prev), blk(cur), blk(nxt), blk(prev), blk(cur), blk(nxt),
            pl.BlockSpec((None, N_HEADS, tq, 3 * tq), variant),
            pl.BlockSpec((1, D_ATTN), lambda bi, i: (0, 0)),
        ],
        out_specs=blk(cur),
        out_shape=jax.ShapeDtypeStruct((b, t, D_ATTN), jnp.bfloat16),
        compiler_params=_params(2),
        name="natten",
    )(q, k, k, k, v, v, v, bias, og)


def _natten_bias(rpb, rows):
    tq = Q_ROWS * GRID_W
    nh, n_dr, n_dc = rpb.shape
    span = 2 * GRID_W - 1
    lo = (GRID_W - 1) - (WIN_COLS - 1)
    padded = jnp.pad(rpb.astype(jnp.float32), ((0, 0), (0, 0), (lo, span - n_dc - lo)))
    flat = jnp.tile(padded, (1, 1, GRID_W))[:, :, GRID_W - 1:GRID_W - 1 + GRID_W * (span - 1)]
    col_t = flat.reshape(nh, n_dr, GRID_W, span - 1)[..., :GRID_W]
    per_a = [col_t[:, WIN_ROWS - 1 - Q_ROWS - a:WIN_ROWS - 1 - Q_ROWS - a + 3 * Q_ROWS]
             for a in range(Q_ROWS)]
    tab = jnp.stack(per_a, axis=1)
    tab = tab.transpose(0, 1, 3, 2, 4).reshape(nh, tq, 3 * tq)

    a = (np.arange(tq) // GRID_W)[:, None]
    c = (np.arange(tq) % GRID_W)[:, None]
    i = (np.arange(3 * tq) // GRID_W)[None, :]
    kc = (np.arange(3 * tq) % GRID_W)[None, :]
    cs = np.clip(c - WIN_COLS // 2, 0, GRID_W - WIN_COLS)
    col_ok = (kc >= cs) & (kc < cs + WIN_COLS)
    tabs = []
    for r0, clamp in ((0, True), (0, False), (rows - Q_ROWS, True)):
        r = r0 + a
        kr = r0 - Q_ROWS + i
        rs = r - WIN_ROWS // 2
        if clamp:
            rs = np.clip(rs, 0, rows - WIN_ROWS)
        ok = col_ok & (kr >= rs) & (kr < rs + WIN_ROWS)
        tabs.append(jnp.where(jnp.asarray(ok)[None], tab, NEG))
    return jnp.stack(tabs, axis=0)


def _out_proj_kernel(x_ref, yc_ref, ya_ref, wo_ref, g_ref, wq_ref, sk_ref, x1_ref, h2_ref, st_ref):
    mix = (jnp.dot(yc_ref[...], wo_ref[0:D_CONV, :], preferred_element_type=jnp.float32)
           + jnp.dot(ya_ref[...], wo_ref[D_CONV:D_CONV + D_ATTN, :], preferred_element_type=jnp.float32))
    x1 = x_ref[...] + mix
    x1_ref[...] = x1
    h2 = _rms(x1, g_ref[...])
    h2_ref[...] = h2
    qry = jnp.dot(h2.astype(jnp.bfloat16), wq_ref[...], preferred_element_type=jnp.float32)
    qry = qry.astype(jnp.bfloat16)
    for j in range(2 * PEER_HEADS):
        st_ref[j] = _dot_nt(sk_ref[j], qry[:, j * PEER_DK_HALF:(j + 1) * PEER_DK_HALF])


def _out_proj(x, yc, ya, w_out, g_ffn, w_query, sub_keys, tm=256):
    n = x.shape[0]
    row = lambda i: (i, 0)
    fix = lambda i: (0, 0)
    nq = w_query.shape[1]
    return pl.pallas_call(
        _out_proj_kernel,
        grid=(n // tm,),
        in_specs=[
            pl.BlockSpec((tm, D_MODEL), row),
            pl.BlockSpec((tm, D_CONV), row),
            pl.BlockSpec((tm, D_ATTN), row),
            pl.BlockSpec((D_CONV + D_ATTN, D_MODEL), fix),
            pl.BlockSpec((1, D_MODEL), fix),
            pl.BlockSpec((D_MODEL, nq), fix),
            pl.BlockSpec((2 * PEER_HEADS, PEER_KEYS, PEER_DK_HALF), lambda i: (0, 0, 0)),
        ],
        out_specs=[
            pl.BlockSpec((tm, D_MODEL), row),
            pl.BlockSpec((tm, D_MODEL), row),
            pl.BlockSpec((2 * PEER_HEADS, PEER_KEYS, tm), lambda i: (0, 0, i)),
        ],
        out_shape=[
            jax.ShapeDtypeStruct((n, D_MODEL), jnp.float32),
            jax.ShapeDtypeStruct((n, D_MODEL), jnp.float32),
            jax.ShapeDtypeStruct((2 * PEER_HEADS, PEER_KEYS, n), jnp.float32),
        ],
        compiler_params=_params(),
        name="out_proj",
    )(x, yc, ya, w_out, g_ffn, w_query, sub_keys)


def _top_rows(s, row_id, count):
    big = jnp.int32(2 ** 30)
    vals, ids = [], []
    for _ in range(count):
        m = jnp.max(s, axis=0, keepdims=True)
        sel = jnp.min(jnp.where(s == m, row_id, big), axis=0, keepdims=True)
        vals.append(m)
        ids.append(sel)
        s = jnp.where(row_id == sel, -jnp.inf, s)
    return vals, ids


def _route_kernel(st_ref, idx_ref, gate_ref):
    tn = st_ref.shape[2]
    key_id = lax.broadcasted_iota(jnp.int32, (PEER_KEYS, tn), 0)
    sub8 = lax.broadcasted_iota(jnp.int32, (8, tn), 0)
    sub16 = lax.broadcasted_iota(jnp.int32, (PEER_TOPK, tn), 0)
    big = jnp.int32(2 ** 30)
    e_all, g_all = [], []
    for h in range(PEER_HEADS):
        sv0, si0 = _top_rows(st_ref[2 * h], key_id, PEER_TOPK)
        sv1, si1 = _top_rows(st_ref[2 * h + 1], key_id, PEER_TOPK)
        v1 = jnp.concatenate(sv1, axis=0)
        i1 = jnp.concatenate(si1, axis=0)
        cand, eid, flat = [], [], []
        for i in range(PEER_TOPK // 2):
            nj = PEER_TOPK // (i + 1)
            rows = PEER_TOPK if nj > 8 else 8
            sub = sub16 if rows == PEER_TOPK else sub8
            c = sv0[i] + v1[0:rows]
            if nj < rows:
                c = jnp.where(sub < nj, c, -jnp.inf)
            cand.append(c)
            eid.append(si0[i] * PEER_KEYS + i1[0:rows])
            flat.append(i * PEER_TOPK + sub)
        v0_tail = jnp.concatenate(sv0[PEER_TOPK // 2:], axis=0)
        i0_tail = jnp.concatenate(si0[PEER_TOPK // 2:], axis=0)
        cand.append(v0_tail + sv1[0])
        eid.append(i0_tail * PEER_KEYS + si1[0])
        flat.append((sub8 + PEER_TOPK // 2) * PEER_TOPK)
        cand = jnp.concatenate(cand, axis=0)
        eid = jnp.concatenate(eid, axis=0)
        flat = jnp.concatenate(flat, axis=0)
        cvs, es = [], []
        for _ in range(PEER_TOPK):
            m = jnp.max(cand, axis=0, keepdims=True)
            fsel = jnp.min(jnp.where(cand == m, flat, big), axis=0, keepdims=True)
            hit = flat == fsel
            es.append(jnp.sum(jnp.where(hit, eid, 0), axis=0, keepdims=True))
            cvs.append(m)
            cand = jnp.where(hit, -jnp.inf, cand)
        cv = jnp.concatenate(cvs, axis=0)
        ex = jnp.exp(cv - cvs[0])
        g_all.append(ex / jnp.sum(ex, axis=0, keepdims=True))
        e_all.append(jnp.concatenate(es, axis=0))
    e_all = jnp.concatenate(e_all, axis=0) * ROW_PLANES
    g_all = jnp.concatenate(g_all, axis=0)
    idx_ref[...] = pltpu.bitcast(pltpu.bitcast(e_all, jnp.float32).T, jnp.int32)
    gate_ref[...] = g_all.T


def _route(st, tn=256):
    n = st.shape[2]
    return pl.pallas_call(
        _route_kernel,
        grid=(n // tn,),
        in_specs=[pl.BlockSpec((2 * PEER_HEADS, PEER_KEYS, tn), lambda i: (0, 0, i))],
        out_specs=[pl.BlockSpec((tn, N_SEL), lambda i: (i, 0))] * 2,
        out_shape=[jax.ShapeDtypeStruct((n, N_SEL), jnp.int32),
                   jax.ShapeDtypeStruct((n, N_SEL), jnp.float32)],
        compiler_params=_params(),
        name="route",
    )(st)


def _load_table(tab_hbm, tab_vmem, sem):
    @pl.when(pl.program_id(0) == 0)
    def _():
        cp = pltpu.make_async_copy(tab_hbm, tab_vmem, sem)
        cp.start()
        cp.wait()


N_OFF = 8


def _gather_rows(row_ref, tab_ref, stage_ref, offs):
    for j in range(N_SEL // N_OFF):
        sub = row_ref.at[pl.ds(j * N_OFF, N_OFF)]
        for i in range(N_OFF):
            k = j * N_OFF + i
            r = pl.multiple_of(sub[offs[i]], ROW_PLANES)
            stage_ref[pl.ds(k, ROW_PLANES, stride=PLANE_ROWS), :] = tab_ref[pl.ds(r, ROW_PLANES), :]


def _staged_matrix(stage_ref):
    los, his = [], []
    for s in range(ROW_PLANES):
        p = stage_ref[pl.ds(PLANE_ROWS * s, N_SEL), :]
        los.append(pltpu.bitcast(p << 16, jnp.float32))
        his.append(pltpu.bitcast(p & jnp.int32(-65536), jnp.float32))
    return jnp.concatenate(los + his, axis=1).astype(jnp.bfloat16)


def _split_rows(row):
    hi = row.astype(jnp.bfloat16)
    lo = (row - hi.astype(jnp.float32)).astype(jnp.bfloat16)
    return jnp.concatenate([hi, lo, jnp.zeros((6, row.shape[1]), jnp.bfloat16)], axis=0)


def _token_pipeline(n_tok, idx_ref, tab_ref, stages, issue, finish, placeholder):
    g = len(stages)
    zero = jnp.minimum(pl.program_id(0), 0)
    offs = [zero + i for i in range(N_OFF)]
    for s in range(g):
        _gather_rows(idx_ref.at[s], tab_ref, stages[s], offs)

    def group(i, prev):
        c = g * i
        for s in range(g):
            finish(jnp.maximum(c + s - g, 0), s, prev[s])
        cur = tuple(issue(c + s, s) for s in range(g))
        for s in range(g):
            _gather_rows(idx_ref.at[jnp.minimum(c + s + g, n_tok - 1)], tab_ref, stages[s], offs)
        return cur

    last = lax.fori_loop(0, n_tok // g, group, (placeholder,) * g)
    for s in range(g):
        finish(n_tok - g + s, s, last[s])


def _peer_u_kernel(idx_ref, h_ref, gate_ref, ut_hbm, w_ref, ut_vmem, *rest):
    stages, sem = rest[:-1], rest[-1]
    _load_table(ut_hbm, ut_vmem, sem)

    def issue(c, slot):
        return _dot_nt(_split_rows(h_ref[pl.ds(c, 1), :]), _staged_matrix(stages[slot]))

    def finish(c, slot, a8):
        a = a8[0:1] + a8[1:2]
        act = 0.5 * a * (1.0 + lax.erf(a * np.float32(1.0 / np.sqrt(2.0))))
        w_ref[pl.ds(c, 1), :] = gate_ref[pl.ds(c, 1), :] * act

    _token_pipeline(h_ref.shape[0], idx_ref, ut_vmem, stages, issue, finish,
                    jnp.zeros((8, N_SEL), jnp.float32))


def _peer_v_kernel(idx_ref, w_ref, x_ref, vt_hbm, out_ref, vt_vmem, *rest):
    stages, sem = rest[:-1], rest[-1]
    _load_table(vt_hbm, vt_vmem, sem)

    def issue(c, slot):
        w_col = jnp.broadcast_to(w_ref[pl.ds(c, 1), :], (N_SEL, N_SEL)).T
        los, his = [], []
        for s in range(ROW_PLANES):
            p = stages[slot][pl.ds(PLANE_ROWS * s, N_SEL), :]
            lo = pltpu.bitcast(p << 16, jnp.float32)
            hi = pltpu.bitcast(p & jnp.int32(-65536), jnp.float32)
            los.append(jnp.sum(lo * w_col, axis=0, keepdims=True))
            his.append(jnp.sum(hi * w_col, axis=0, keepdims=True))
        return jnp.concatenate(los + his, axis=1)

    def finish(c, slot, peer):
        out_ref[pl.ds(c, 1), :] = x_ref[pl.ds(c, 1), :] + peer

    _token_pipeline(x_ref.shape[0], idx_ref, vt_vmem, stages, issue, finish,
                    jnp.zeros((1, D_MODEL), jnp.float32))


def _peer_call(kernel_fn, name, idx, a, b, table, out_cols, group, tm=256):
    n = idx.shape[0]
    row = lambda i: (i, 0)
    return pl.pallas_call(
        kernel_fn,
        grid=(n // tm,),
        in_specs=[
            pl.BlockSpec((tm, N_SEL), row, memory_space=pltpu.SMEM),
            pl.BlockSpec((tm, a.shape[1]), row),
            pl.BlockSpec((tm, b.shape[1]), row),
            pl.BlockSpec(memory_space=pl.ANY),
        ],
        out_specs=pl.BlockSpec((tm, out_cols), row),
        out_shape=jax.ShapeDtypeStruct((n, out_cols), jnp.float32),
        scratch_shapes=[pltpu.VMEM((N_EXPERTS * ROW_PLANES, LANES), jnp.int32)]
        + [pltpu.VMEM((ROW_PLANES * PLANE_ROWS, LANES), jnp.int32)] * group
        + [pltpu.SemaphoreType.DMA],
        compiler_params=_params(),
        name=name,
    )(idx, a, b, table)


def _pack_table(t):
    tb = t.astype(jnp.bfloat16)
    lo = lax.bitcast_convert_type(tb[:, :ROW_WORDS], jnp.uint16).astype(jnp.uint32)
    hi = lax.bitcast_convert_type(tb[:, ROW_WORDS:], jnp.uint16).astype(jnp.uint32)
    words = lax.bitcast_convert_type(lo | (hi << 16), jnp.int32)
    return words.reshape(t.shape[0] * ROW_PLANES, LANES)


def _layer(x, p):
    b, t, _ = x.shape
    n = b * t
    xf = x.reshape(n, D_MODEL)
    u, q, k, v = _in_proj(xf, p["g_mix"], p["w_in"], p["hsum"], p["qg"], p["kg"])
    seq = lambda z: z.reshape(b, t, z.shape[-1])
    yc = _conv(seq(u), p["conv_w"], p["conv_b"], p["conv_ln_g"], p["conv_ln_b"], p["g_out_conv"])
    bias = _natten_bias(p["rpb"], t // GRID_W)
    ya = _natten(seq(q), seq(k), seq(v), bias, p["g_out_attn"])
    x1, h2, st = _out_proj(xf, yc.reshape(n, D_CONV), ya.reshape(n, D_ATTN), p["w_out"], p["g_ffn"],
                           p["w_query"], p["sub_keys"])
    idx, gate = _route(st)
    w = _peer_call(_peer_u_kernel, "peer_u", idx, h2, gate, p["ut"], N_SEL, group=4)
    y = _peer_call(_peer_v_kernel, "peer_v", idx, w, x1, p["vt"], D_MODEL, group=4)
    return y.reshape(b, t, D_MODEL)


def kernel(x_prompt, x_sample, g_mix, w_in, conv_w, conv_b, conv_ln_g, conv_ln_b, q_norm_g, k_norm_g, rpb, g_out_conv, g_out_attn, w_out, g_ffn, w_query, sub_keys, expert_u, expert_v):
    depth = g_mix.shape[0]
    head = np.arange(D_ATTN) // HEAD_DIM
    hsum = jnp.asarray((head[:, None] == head[None, :]) / HEAD_DIM, jnp.bfloat16)
    scale = HEAD_DIM ** -0.5
    y_prompt, y_sample = x_prompt, x_sample
    for l in range(depth):
        row = lambda z: z[l].reshape(1, -1)
        p = dict(
            g_mix=row(g_mix), w_in=w_in[l].astype(jnp.bfloat16), hsum=hsum,
            qg=jnp.tile(q_norm_g[l], N_HEADS).reshape(1, -1) * scale,
            kg=jnp.tile(k_norm_g[l], N_HEADS).reshape(1, -1),
            conv_w=conv_w[l], conv_b=row(conv_b), conv_ln_g=row(conv_ln_g), conv_ln_b=row(conv_ln_b),
            g_out_conv=row(g_out_conv), g_out_attn=row(g_out_attn), rpb=rpb[l],
            w_out=w_out[l].astype(jnp.bfloat16), g_ffn=row(g_ffn),
            w_query=w_query[l].astype(jnp.bfloat16),
            sub_keys=sub_keys[l].reshape(2 * PEER_HEADS, PEER_KEYS, PEER_DK_HALF).astype(jnp.bfloat16),
            ut=_pack_table(expert_u[l]), vt=_pack_table(expert_v[l]),
        )
        y_prompt = _layer(y_prompt, p)
        y_sample = _layer(y_sample, p)
    return (y_prompt, y_sample)
```

```python
import functools

import numpy as np
import jax
import jax.numpy as jnp
from jax import lax
from jax.experimental import pallas as pl
from jax.experimental.pallas import tpu as pltpu

EPS = 1e-6
NEG = -1e30

D_MODEL = 1024
D_CONV = 512
D_ATTN = 512
N_HEADS = 8
HEAD_DIM = 64
CONV_K = 31
GRID_W = 64
WIN_ROWS = 8
WIN_COLS = 16

PEER_HEADS = 8
PEER_KEYS = 128
PEER_DK_HALF = 128
PEER_TOPK = 16
N_SEL = PEER_HEADS * PEER_TOPK
N_EXPERTS = PEER_KEYS * PEER_KEYS

LANES = 128
SUBLANES = 8
ROW_WORDS = D_MODEL // 2
ROW_PLANES = ROW_WORDS // LANES
PLANE_ROWS = N_SEL + 8

Q_ROWS = 4
HALO = 16

VMEM_LIMIT = 56 * 1024 * 1024


def _params(n_axes=1, flags=None):
    return pltpu.CompilerParams(
        dimension_semantics=("arbitrary",) * n_axes, vmem_limit_bytes=VMEM_LIMIT, flags=flags)


def _rms(x, g):
    return x * lax.rsqrt(jnp.mean(x * x, axis=-1, keepdims=True) + EPS) * g


def _dot_nt(a, b):
    return lax.dot_general(a, b, (((1,), (1,)), ((), ())), preferred_element_type=jnp.float32)


def _in_proj_kernel(x_ref, g_ref, w_ref, hsum_ref, qg_ref, kg_ref, u_ref, q_ref, k_ref, v_ref):
    h = _rms(x_ref[...], g_ref[...])
    proj = jnp.dot(h.astype(jnp.bfloat16), w_ref[...], preferred_element_type=jnp.float32)
    a = proj[:, 0:D_CONV]
    gate = proj[:, D_CONV:2 * D_CONV]
    u_ref[...] = a * jax.nn.sigmoid(gate)

    def head_rms(t, gain):
        tt = t * t
        hi = tt.astype(jnp.bfloat16)
        lo = (tt - hi.astype(jnp.float32)).astype(jnp.bfloat16)
        ms = (jnp.dot(hi, hsum_ref[...], preferred_element_type=jnp.float32)
              + jnp.dot(lo, hsum_ref[...], preferred_element_type=jnp.float32))
        return t * lax.rsqrt(ms + EPS) * gain

    o = 2 * D_CONV
    q_ref[...] = head_rms(proj[:, o:o + D_ATTN], qg_ref[...]).astype(jnp.bfloat16)
    k_ref[...] = head_rms(proj[:, o + D_ATTN:o + 2 * D_ATTN], kg_ref[...]).astype(jnp.bfloat16)
    v_ref[...] = proj[:, o + 2 * D_ATTN:o + 3 * D_ATTN].astype(jnp.bfloat16)


def _in_proj(x, g_mix, w_in, hsum, qg, kg, tm=512):
    n = x.shape[0]
    d_in = w_in.shape[1]
    row = lambda i: (i, 0)
    fix = lambda i: (0, 0)
    return pl.pallas_call(
        _in_proj_kernel,
        grid=(n // tm,),
        in_specs=[
            pl.BlockSpec((tm, D_MODEL), row),
            pl.BlockSpec((1, D_MODEL), fix),
            pl.BlockSpec((D_MODEL, d_in), fix),
            pl.BlockSpec((D_ATTN, D_ATTN), fix),
            pl.BlockSpec((1, D_ATTN), fix),
            pl.BlockSpec((1, D_ATTN), fix),
        ],
        out_specs=[pl.BlockSpec((tm, D_CONV), row)] + [pl.BlockSpec((tm, D_ATTN), row)] * 3,
        out_shape=[jax.ShapeDtypeStruct((n, D_CONV), jnp.float32)]
        + [jax.ShapeDtypeStruct((n, D_ATTN), jnp.bfloat16)] * 3,
        compiler_params=_params(),
        name="in_proj",
    )(x, g_mix, w_in, hsum, qg, kg)


def _conv_kernel(prev_ref, cur_ref, next_ref, w_ref, b_ref, lg_ref, lb_ref, og_ref, out_ref, xp_ref,
                 *shift_refs, chunk):
    t = pl.program_id(1)
    nt = pl.num_programs(1)
    tt = cur_ref.shape[0]
    xp_ref[0:HALO, :] = jnp.where(t > 0, prev_ref[...], 0.0)
    xp_ref[HALO:HALO + tt, :] = cur_ref[...]
    xp_ref[HALO + tt:HALO + tt + HALO, :] = jnp.where(t < nt - 1, next_ref[...], 0.0)

    span = chunk + 2 * HALO - SUBLANES
    first_tap = HALO - CONV_K // 2

    def one_chunk(base, shift_ref):
        win = xp_ref[pl.ds(base, chunk + 2 * HALO), :]
        for r in range(1, SUBLANES):
            shift_ref[r] = win[r:r + span, :]
        acc = jnp.zeros((chunk, D_CONV), jnp.float32)
        for r in range(SUBLANES):
            for a in range(span // SUBLANES):
                j = SUBLANES * a + r - first_tap
                if 0 <= j < CONV_K and SUBLANES * a + chunk <= span:
                    rows = slice(SUBLANES * a, SUBLANES * a + chunk)
                    tap = win[rows, :] if r == 0 else shift_ref[r, rows, :]
                    acc = acc + tap * w_ref[j:j + 1, :]
        y = acc + b_ref[...]
        mu = jnp.mean(y, axis=-1, keepdims=True)
        yc = y - mu
        yn = yc * lax.rsqrt(jnp.mean(yc * yc, axis=-1, keepdims=True) + EPS) * lg_ref[...] + lb_ref[...]
        z = yn * jax.nn.sigmoid(yn)
        out_ref[pl.ds(base, chunk), :] = _rms(z, og_ref[...]).astype(out_ref.dtype)

    def body(i, carry):
        for half, shift_ref in enumerate(shift_refs):
            one_chunk(pl.multiple_of((2 * i + half) * chunk, chunk), shift_ref)
        return carry

    lax.fori_loop(0, tt // (2 * chunk), body, 0)


def _conv(u, conv_w, conv_b, ln_g, ln_b, og, tt=512, chunk=32):
    b, t, _ = u.shape
    hb = tt // HALO
    n_halo = t // HALO
    fix = lambda bi, ti: (0, 0)
    return pl.pallas_call(
        functools.partial(_conv_kernel, chunk=chunk),
        grid=(b, t // tt),
        in_specs=[
            pl.BlockSpec((None, HALO, D_CONV), lambda bi, ti: (bi, jnp.maximum(ti * hb - 1, 0), 0)),
            pl.BlockSpec((None, tt, D_CONV), lambda bi, ti: (bi, ti, 0)),
            pl.BlockSpec((None, HALO, D_CONV),
                         lambda bi, ti: (bi, jnp.minimum((ti + 1) * hb, n_halo - 1), 0)),
            pl.BlockSpec((CONV_K, D_CONV), fix),
            pl.BlockSpec((1, D_CONV), fix),
            pl.BlockSpec((1, D_CONV), fix),
            pl.BlockSpec((1, D_CONV), fix),
            pl.BlockSpec((1, D_CONV), fix),
        ],
        out_specs=pl.BlockSpec((None, tt, D_CONV), lambda bi, ti: (bi, ti, 0)),
        out_shape=jax.ShapeDtypeStruct((b, t, D_CONV), jnp.bfloat16),
        scratch_shapes=[pltpu.VMEM((tt + 2 * HALO, D_CONV), jnp.float32)]
        + [pltpu.VMEM((SUBLANES, chunk + 2 * HALO - SUBLANES, D_CONV), jnp.float32)] * 2,
        compiler_params=_params(2),
        name="conv",
    )(u, u, u, conv_w, conv_b, ln_g, ln_b, og)


def _natten_kernel(q_ref, kp_ref, kc_ref, kn_ref, vp_ref, vc_ref, vn_ref, bias_ref, og_ref, out_ref):
    lane = lax.broadcasted_iota(jnp.int32, (1, LANES), 1)
    first_head = lane < HEAD_DIM
    outs = []
    for p in range(N_HEADS // 2):
        sl = slice(p * LANES, (p + 1) * LANES)
        q2 = q_ref[:, sl]
        kcat = jnp.concatenate([kp_ref[:, sl], kc_ref[:, sl], kn_ref[:, sl]], axis=0)
        vcat = jnp.concatenate([vp_ref[:, sl], vc_ref[:, sl], vn_ref[:, sl]], axis=0)
        o_pair = []
        for sub in range(2):
            mine = first_head if sub == 0 else jnp.logical_not(first_head)
            qm = jnp.where(mine, q2, jnp.zeros_like(q2))
            s = _dot_nt(qm, kcat) + bias_ref[2 * p + sub]
            m = jnp.max(s, axis=-1, keepdims=True)
            e = jnp.exp(s - m)
            l = jnp.sum(e, axis=-1, keepdims=True)
            o = jnp.dot(e.astype(jnp.bfloat16), vcat, preferred_element_type=jnp.float32)
            o_pair.append(o / l)
        outs.append(jnp.where(first_head, o_pair[0], o_pair[1]))
    y = jnp.concatenate(outs, axis=1)
    out_ref[...] = _rms(y, og_ref[...]).astype(out_ref.dtype)


def _natten(q, k, v, bias, og):
    b, t, _ = q.shape
    tq = Q_ROWS * GRID_W
    nb = t // tq
    prev = lambda bi, i: (bi, jnp.maximum(i - 1, 0), 0)
    cur = lambda bi, i: (bi, i, 0)
    nxt = lambda bi, i: (bi, jnp.minimum(i + 1, nb - 1), 0)
    blk = lambda m: pl.BlockSpec((None, tq, D_ATTN), m)
    variant = lambda bi, i: (jnp.where(i == 0, 0, jnp.where(i == nb - 1, 2, 1)), 0, 0, 0)
    return pl.pallas_call(
        _natten_kernel,
        grid=(b, nb),
        in_specs=[
            blk(cur), blk(prev), blk(cur), blk(nxt), blk(prev), blk(cur), blk(nxt),
            pl.BlockSpec((None, N_HEADS, tq, 3 * tq), variant),
            pl.BlockSpec((1, D_ATTN), lambda bi, i: (0, 0)),
        ],
        out_specs=blk(cur),
        out_shape=jax.ShapeDtypeStruct((b, t, D_ATTN), jnp.bfloat16),
        compiler_params=_params(2),
        name="natten",
    )(q, k, k, k, v, v, v, bias, og)


def _natten_bias(rpb, rows):
    tq = Q_ROWS * GRID_W
    nh, n_dr, n_dc = rpb.shape
    span = 2 * GRID_W - 1
    lo = (GRID_W - 1) - (WIN_COLS - 1)
    padded = jnp.pad(rpb.astype(jnp.float32), ((0, 0), (0, 0), (lo, span - n_dc - lo)))
    flat = jnp.tile(padded, (1, 1, GRID_W))[:, :, GRID_W - 1:GRID_W - 1 + GRID_W * (span - 1)]
    col_t = flat.reshape(nh, n_dr, GRID_W, span - 1)[..., :GRID_W]
    per_a = [col_t[:, WIN_ROWS - 1 - Q_ROWS - a:WIN_ROWS - 1 - Q_ROWS - a + 3 * Q_ROWS]
             for a in range(Q_ROWS)]
    tab = jnp.stack(per_a, axis=1)
    tab = tab.transpose(0, 1, 3, 2, 4).reshape(nh, tq, 3 * tq)

    a = (np.arange(tq) // GRID_W)[:, None]
    c = (np.arange(tq) % GRID_W)[:, None]
    i = (np.arange(3 * tq) // GRID_W)[None, :]
    kc = (np.arange(3 * tq) % GRID_W)[None, :]
    cs = np.clip(c - WIN_COLS // 2, 0, GRID_W - WIN_COLS)
    col_ok = (kc >= cs) & (kc < cs + WIN_COLS)
    tabs = []
    for r0, clamp in ((0, True), (0, False), (rows - Q_ROWS, True)):
        r = r0 + a
        kr = r0 - Q_ROWS + i
        rs = r - WIN_ROWS // 2
        if clamp:
            rs = np.clip(rs, 0, rows - WIN_ROWS)
        ok = col_ok & (kr >= rs) & (kr < rs + WIN_ROWS)
        tabs.append(jnp.where(jnp.asarray(ok)[None], tab, NEG))
    return jnp.stack(tabs, axis=0)


def _out_proj_kernel(x_ref, yc_ref, ya_ref, wo_ref, g_ref, wq_ref, sk_ref, x1_ref, h2_ref, st_ref):
    mix = (jnp.dot(yc_ref[...], wo_ref[0:D_CONV, :], preferred_element_type=jnp.float32)
           + jnp.dot(ya_ref[...], wo_ref[D_CONV:D_CONV + D_ATTN, :], preferred_element_type=jnp.float32))
    x1 = x_ref[...] + mix
    x1_ref[...] = x1
    h2 = _rms(x1, g_ref[...])
    h2_ref[...] = h2
    qry = jnp.dot(h2.astype(jnp.bfloat16), wq_ref[...], preferred_element_type=jnp.float32)
    qry = qry.astype(jnp.bfloat16)
    for j in range(2 * PEER_HEADS):
        st_ref[j] = _dot_nt(sk_ref[j], qry[:, j * PEER_DK_HALF:(j + 1) * PEER_DK_HALF])


def _out_proj(x, yc, ya, w_out, g_ffn, w_query, sub_keys, tm=256):
    n = x.shape[0]
    row = lambda i: (i, 0)
    fix = lambda i: (0, 0)
    nq = w_query.shape[1]
    return pl.pallas_call(
        _out_proj_kernel,
        grid=(n // tm,),
        in_specs=[
            pl.BlockSpec((tm, D_MODEL), row),
            pl.BlockSpec((tm, D_CONV), row),
            pl.BlockSpec((tm, D_ATTN), row),
            pl.BlockSpec((D_CONV + D_ATTN, D_MODEL), fix),
            pl.BlockSpec((1, D_MODEL), fix),
            pl.BlockSpec((D_MODEL, nq), fix),
            pl.BlockSpec((2 * PEER_HEADS, PEER_KEYS, PEER_DK_HALF), lambda i: (0, 0, 0)),
        ],
        out_specs=[
            pl.BlockSpec((tm, D_MODEL), row),
            pl.BlockSpec((tm, D_MODEL), row),
            pl.BlockSpec((2 * PEER_HEADS, PEER_KEYS, tm), lambda i: (0, 0, i)),
        ],
        out_shape=[
            jax.ShapeDtypeStruct((n, D_MODEL), jnp.float32),
            jax.ShapeDtypeStruct((n, D_MODEL), jnp.float32),
            jax.ShapeDtypeStruct((2 * PEER_HEADS, PEER_KEYS, n), jnp.float32),
        ],
        compiler_params=_params(),
        name="out_proj",
    )(x, yc, ya, w_out, g_ffn, w_query, sub_keys)


BIG_ID = 2.0 ** 30


def _top_rows(s, row_id, slot, count):
    vals = jnp.zeros(slot.shape, jnp.float32)
    ids = jnp.zeros(slot.shape, jnp.float32)
    for r in range(count):
        m = jnp.max(s, axis=0, keepdims=True)
        sel = jnp.min(jnp.where(s == m, row_id, BIG_ID), axis=0, keepdims=True)
        vals = jnp.where(slot == r, m, vals)
        ids = jnp.where(slot == r, sel, ids)
        s = jnp.where(row_id == sel, -jnp.inf, s)
    return vals, ids


def _route_kernel(st_ref, idx_ref, gate_ref):
    for t0 in range(0, st_ref.shape[2], LANES):
        _route_tile(st_ref, idx_ref, gate_ref, t0)


def _route_tile(st_ref, idx_ref, gate_ref, t0):
    tn = LANES
    half = PEER_TOPK // 2
    key_id = lax.broadcasted_iota(jnp.int32, (PEER_KEYS, tn), 0).astype(jnp.float32)
    slot = lax.broadcasted_iota(jnp.int32, (PEER_TOPK, tn), 0)
    sub16 = slot.astype(jnp.float32)
    sub8 = sub16[0:half]
    row_scale0 = float(PEER_KEYS * ROW_PLANES)
    row_scale1 = float(ROW_PLANES)
    e_all, g_all = [], []
    for h in range(PEER_HEADS):
        v0, i0 = _top_rows(st_ref[2 * h, :, t0:t0 + tn], key_id, slot, PEER_TOPK)
        v1, i1 = _top_rows(st_ref[2 * h + 1, :, t0:t0 + tn], key_id, slot, PEER_TOPK)
        cand, row, flat = [], [], []
        for i in range(half):
            nj = PEER_TOPK // (i + 1)
            rows = PEER_TOPK if nj > half else half
            sub = sub16 if rows == PEER_TOPK else sub8
            c = v0[i:i + 1] + v1[0:rows]
            if nj < rows:
                c = jnp.where(sub < nj, c, -jnp.inf)
            cand.append(c)
            row.append(i0[i:i + 1] * row_scale0 + i1[0:rows] * row_scale1)
            flat.append(sub + float(i * PEER_TOPK))
        cand.append(v0[half:] + v1[0:1])
        row.append(i0[half:] * row_scale0 + i1[0:1] * row_scale1)
        flat.append((sub8 + float(half)) * float(PEER_TOPK))
        cand = jnp.concatenate(cand, axis=0)
        row = jnp.concatenate(row, axis=0)
        flat = jnp.concatenate(flat, axis=0)
        cv = jnp.zeros((PEER_TOPK, tn), jnp.float32)
        picked = jnp.zeros((PEER_TOPK, tn), jnp.float32)
        for r in range(PEER_TOPK):
            m = jnp.max(cand, axis=0, keepdims=True)
            fsel = jnp.min(jnp.where(cand == m, flat, BIG_ID), axis=0, keepdims=True)
            hit = flat == fsel
            got = jnp.sum(jnp.where(hit, row, 0.0), axis=0, keepdims=True)
            cv = jnp.where(slot == r, m, cv)
            picked = jnp.where(slot == r, got, picked)
            cand = jnp.where(hit, -jnp.inf, cand)
        ex = jnp.exp(cv - cv[0:1])
        g_all.append(ex / jnp.sum(ex, axis=0, keepdims=True))
        e_all.append(picked)
    e_all = jnp.concatenate(e_all, axis=0)
    g_all = jnp.concatenate(g_all, axis=0)
    idx_ref[t0:t0 + tn, :] = e_all.T.astype(jnp.int32)
    gate_ref[t0:t0 + tn, :] = g_all.T


def _route(st, tn=256):
    n = st.shape[2]
    return pl.pallas_call(
        _route_kernel,
        grid=(n // tn,),
        in_specs=[pl.BlockSpec((2 * PEER_HEADS, PEER_KEYS, tn), lambda i: (0, 0, i))],
        out_specs=[pl.BlockSpec((tn, N_SEL), lambda i: (i, 0))] * 2,
        out_shape=[jax.ShapeDtypeStruct((n, N_SEL), jnp.int32),
                   jax.ShapeDtypeStruct((n, N_SEL), jnp.float32)],
        compiler_params=_params(),
        name="route",
    )(st)


def _load_table(tab_hbm, tab_vmem, sem):
    @pl.when(pl.program_id(0) == 0)
    def _():
        cp = pltpu.make_async_copy(tab_hbm, tab_vmem, sem)
        cp.start()
        cp.wait()


N_OFF = 8


def _gather_rows(row_ref, tab_ref, stage_ref, offs):
    for j in range(N_SEL // N_OFF):
        sub = row_ref.at[pl.ds(j * N_OFF, N_OFF)]
        for i in range(N_OFF):
            k = j * N_OFF + i
            r = pl.multiple_of(sub[offs[i]], ROW_PLANES)
            stage_ref[pl.ds(k, ROW_PLANES, stride=PLANE_ROWS), :] = tab_ref[pl.ds(r, ROW_PLANES), :]


def _gather_rows_rolled(row_ref, tab_ref, stage_ref):
    def body(j, carry):
        for i in range(N_OFF):
            k = j * N_OFF + i
            r = pl.multiple_of(row_ref[k], ROW_PLANES)
            stage_ref[pl.ds(k, ROW_PLANES, stride=PLANE_ROWS), :] = tab_ref[pl.ds(r, ROW_PLANES), :]
        return carry

    lax.fori_loop(0, N_SEL // N_OFF, body, 0)


def _staged_matrix(stage_ref):
    los, his = [], []
    for s in range(ROW_PLANES):
        p = stage_ref[pl.ds(PLANE_ROWS * s, N_SEL), :]
        los.append(pltpu.bitcast(p << 16, jnp.float32))
        his.append(pltpu.bitcast(p & jnp.int32(-65536), jnp.float32))
    return jnp.concatenate(los + his, axis=1).astype(jnp.bfloat16)


def _split_rows(row):
    hi = row.astype(jnp.bfloat16)
    lo = (row - hi.astype(jnp.float32)).astype(jnp.bfloat16)
    return jnp.concatenate([hi, lo, jnp.zeros((6, row.shape[1]), jnp.bfloat16)], axis=0)


def _token_pipeline(n_tok, idx_ref, tab_ref, stages, issue, finish, placeholder):
    g = len(stages)
    zero = jnp.minimum(pl.program_id(0), 0)
    offs = [zero + i for i in range(N_OFF)]
    for s in range(g):
        _gather_rows_rolled(idx_ref.at[s], tab_ref, stages[s])

    def group(i, prev):
        c = g * i
        for s in range(g):
            finish(jnp.maximum(c + s - g, 0), s, prev[s])
        cur = tuple(issue(c + s, s) for s in range(g))
        for s in range(g):
            _gather_rows(idx_ref.at[jnp.minimum(c + s + g, n_tok - 1)], tab_ref, stages[s], offs)
        return cur

    last = lax.fori_loop(0, n_tok // g, group, (placeholder,) * g)
    for s in range(g):
        finish(n_tok - g + s, s, last[s])


def _peer_u_kernel(idx_ref, h_ref, gate_ref, ut_hbm, w_ref, ut_vmem, *rest):
    stages, sem = rest[:-1], rest[-1]
    _load_table(ut_hbm, ut_vmem, sem)

    def issue(c, slot):
        return _dot_nt(_split_rows(h_ref[pl.ds(c, 1), :]), _staged_matrix(stages[slot]))

    def finish(c, slot, a8):
        a = a8[0:1] + a8[1:2]
        act = 0.5 * a * (1.0 + lax.erf(a * np.float32(1.0 / np.sqrt(2.0))))
        w_ref[pl.ds(c, 1), :] = gate_ref[pl.ds(c, 1), :] * act

    _token_pipeline(h_ref.shape[0], idx_ref, ut_vmem, stages, issue, finish,
                    jnp.zeros((8, N_SEL), jnp.float32))


def _peer_v_kernel(idx_ref, w_ref, x_ref, vt_hbm, out_ref, vt_vmem, *rest):
    stages, sem = rest[:-1], rest[-1]
    _load_table(vt_hbm, vt_vmem, sem)

    def issue(c, slot):
        w_col = jnp.broadcast_to(w_ref[pl.ds(c, 1), :], (N_SEL, N_SEL)).T
        los, his = [], []
        for s in range(ROW_PLANES):
            p = stages[slot][pl.ds(PLANE_ROWS * s, N_SEL), :]
            lo = pltpu.bitcast(p << 16, jnp.float32)
            hi = pltpu.bitcast(p & jnp.int32(-65536), jnp.float32)
            los.append(jnp.sum(lo * w_col, axis=0, keepdims=True))
            his.append(jnp.sum(hi * w_col, axis=0, keepdims=True))
        return jnp.concatenate(los + his, axis=1)

    def finish(c, slot, peer):
        out_ref[pl.ds(c, 1), :] = x_ref[pl.ds(c, 1), :] + peer

    _token_pipeline(x_ref.shape[0], idx_ref, vt_vmem, stages, issue, finish,
                    jnp.zeros((1, D_MODEL), jnp.float32))


def _peer_call(kernel_fn, name, idx, a, b, table, out_cols, group, tm=512):
    n = idx.shape[0]
    row = lambda i: (i, 0)
    return pl.pallas_call(
        kernel_fn,
        grid=(n // tm,),
        in_specs=[
            pl.BlockSpec((tm, N_SEL), row, memory_space=pltpu.SMEM),
            pl.BlockSpec((tm, a.shape[1]), row),
            pl.BlockSpec((tm, b.shape[1]), row),
            pl.BlockSpec(memory_space=pl.ANY),
        ],
        out_specs=pl.BlockSpec((tm, out_cols), row),
        out_shape=jax.ShapeDtypeStruct((n, out_cols), jnp.float32),
        scratch_shapes=[pltpu.VMEM((N_EXPERTS * ROW_PLANES, LANES), jnp.int32)]
        + [pltpu.VMEM((ROW_PLANES * PLANE_ROWS, LANES), jnp.int32)] * group
        + [pltpu.SemaphoreType.DMA],
        compiler_params=_params(),
        name=name,
    )(idx, a, b, table)


def _pack_table(t):
    tb = t.astype(jnp.bfloat16)
    lo = lax.bitcast_convert_type(tb[:, :ROW_WORDS], jnp.uint16).astype(jnp.uint32)
    hi = lax.bitcast_convert_type(tb[:, ROW_WORDS:], jnp.uint16).astype(jnp.uint32)
    words = lax.bitcast_convert_type(lo | (hi << 16), jnp.int32)
    return words.reshape(t.shape[0] * ROW_PLANES, LANES)


def _layer(x, p):
    b, t, _ = x.shape
    n = b * t
    xf = x.reshape(n, D_MODEL)
    u, q, k, v = _in_proj(xf, p["g_mix"], p["w_in"], p["hsum"], p["qg"], p["kg"])
    seq = lambda z: z.reshape(b, t, z.shape[-1])
    yc = _conv(seq(u), p["conv_w"], p["conv_b"], p["conv_ln_g"], p["conv_ln_b"], p["g_out_conv"])
    bias = _natten_bias(p["rpb"], t // GRID_W)
    ya = _natten(seq(q), seq(k), seq(v), bias, p["g_out_attn"])
    x1, h2, st = _out_proj(xf, yc.reshape(n, D_CONV), ya.reshape(n, D_ATTN), p["w_out"], p["g_ffn"],
                           p["w_query"], p["sub_keys"])
    idx, gate = _route(st)
    w = _peer_call(_peer_u_kernel, "peer_u", idx, h2, gate, p["ut"], N_SEL, group=8)
    y = _peer_call(_peer_v_kernel, "peer_v", idx, w, x1, p["vt"], D_MODEL, group=8)
    return y.reshape(b, t, D_MODEL)


def kernel(x_prompt, x_sample, g_mix, w_in, conv_w, conv_b, conv_ln_g, conv_ln_b, q_norm_g, k_norm_g, rpb, g_out_conv, g_out_attn, w_out, g_ffn, w_query, sub_keys, expert_u, expert_v):
    depth = g_mix.shape[0]
    head = np.arange(D_ATTN) // HEAD_DIM
    hsum = jnp.asarray((head[:, None] == head[None, :]) / HEAD_DIM, jnp.bfloat16)
    scale = HEAD_DIM ** -0.5
    y_prompt, y_sample = x_prompt, x_sample
    for l in range(depth):
        row = lambda z: z[l].reshape(1, -1)
        p = dict(
            g_mix=row(g_mix), w_in=w_in[l].astype(jnp.bfloat16), hsum=hsum,
            qg=jnp.tile(q_norm_g[l], N_HEADS).reshape(1, -1) * scale,
            kg=jnp.tile(k_norm_g[l], N_HEADS).reshape(1, -1),
            conv_w=conv_w[l], conv_b=row(conv_b), conv_ln_g=row(conv_ln_g), conv_ln_b=row(conv_ln_b),
            g_out_conv=row(g_out_conv), g_out_attn=row(g_out_attn), rpb=rpb[l],
            w_out=w_out[l].astype(jnp.bfloat16), g_ffn=row(g_ffn),
            w_query=w_query[l].astype(jnp.bfloat16),
            sub_keys=sub_keys[l].reshape(2 * PEER_HEADS, PEER_KEYS, PEER_DK_HALF).astype(jnp.bfloat16),
            ut=_pack_table(expert_u[l]), vt=_pack_table(expert_v[l]),
        )
        y_prompt = _layer(y_prompt, p)
        y_sample = _layer(y_sample, p)
    return (y_prompt, y_sample)
```

```python
import functools

import numpy as np
import jax
import jax.numpy as jnp
from jax import lax
from jax.experimental import pallas as pl
from jax.experimental.pallas import tpu as pltpu

EPS = 1e-6
NEG = -1e30

D_MODEL = 1024
D_CONV = 512
D_ATTN = 512
N_HEADS = 8
HEAD_DIM = 64
CONV_K = 31
GRID_W = 64
WIN_ROWS = 8
WIN_COLS = 16

PEER_HEADS = 8
PEER_KEYS = 128
PEER_DK_HALF = 128
PEER_TOPK = 16
N_SEL = PEER_HEADS * PEER_TOPK
N_EXPERTS = PEER_KEYS * PEER_KEYS

LANES = 128
SUBLANES = 8
ROW_WORDS = D_MODEL // 2
ROW_PLANES = ROW_WORDS // LANES
PLANE_ROWS = N_SEL + 8

Q_ROWS = 4
HALO = 16

VMEM_LIMIT = 56 * 1024 * 1024


def _params(n_axes=1, flags=None):
    return pltpu.CompilerParams(
        dimension_semantics=("arbitrary",) * n_axes, vmem_limit_bytes=VMEM_LIMIT, flags=flags)


def _rms(x, g):
    return x * lax.rsqrt(jnp.mean(x * x, axis=-1, keepdims=True) + EPS) * g


def _dot_nt(a, b):
    return lax.dot_general(a, b, (((1,), (1,)), ((), ())), preferred_element_type=jnp.float32)


def _in_proj_kernel(x_ref, g_ref, w_ref, hsum_ref, qg_ref, kg_ref, u_ref, q_ref, k_ref, v_ref):
    h = _rms(x_ref[...], g_ref[...])
    proj = jnp.dot(h.astype(jnp.bfloat16), w_ref[...], preferred_element_type=jnp.float32)
    a = proj[:, 0:D_CONV]
    gate = proj[:, D_CONV:2 * D_CONV]
    u_ref[...] = a * jax.nn.sigmoid(gate)

    def head_rms(t, gain):
        tt = t * t
        hi = tt.astype(jnp.bfloat16)
        lo = (tt - hi.astype(jnp.float32)).astype(jnp.bfloat16)
        ms = (jnp.dot(hi, hsum_ref[...], preferred_element_type=jnp.float32)
              + jnp.dot(lo, hsum_ref[...], preferred_element_type=jnp.float32))
        return t * lax.rsqrt(ms + EPS) * gain

    o = 2 * D_CONV
    q_ref[...] = head_rms(proj[:, o:o + D_ATTN], qg_ref[...]).astype(jnp.bfloat16)
    k_ref[...] = head_rms(proj[:, o + D_ATTN:o + 2 * D_ATTN], kg_ref[...]).astype(jnp.bfloat16)
    v_ref[...] = proj[:, o + 2 * D_ATTN:o + 3 * D_ATTN].astype(jnp.bfloat16)


def _in_proj(x, g_mix, w_in, hsum, qg, kg, tm=512):
    n = x.shape[0]
    d_in = w_in.shape[1]
    row = lambda i: (i, 0)
    fix = lambda i: (0, 0)
    return pl.pallas_call(
        _in_proj_kernel,
        grid=(n // tm,),
        in_specs=[
            pl.BlockSpec((tm, D_MODEL), row),
            pl.BlockSpec((1, D_MODEL), fix),
            pl.BlockSpec((D_MODEL, d_in), fix),
            pl.BlockSpec((D_ATTN, D_ATTN), fix),
            pl.BlockSpec((1, D_ATTN), fix),
            pl.BlockSpec((1, D_ATTN), fix),
        ],
        out_specs=[pl.BlockSpec((tm, D_CONV), row)] + [pl.BlockSpec((tm, D_ATTN), row)] * 3,
        out_shape=[jax.ShapeDtypeStruct((n, D_CONV), jnp.float32)]
        + [jax.ShapeDtypeStruct((n, D_ATTN), jnp.bfloat16)] * 3,
        compiler_params=_params(),
        name="in_proj",
    )(x, g_mix, w_in, hsum, qg, kg)


def _conv_kernel(prev_ref, cur_ref, next_ref, w_ref, b_ref, lg_ref, lb_ref, og_ref, out_ref, xp_ref,
                 *shift_refs, chunk):
    t = pl.program_id(1)
    nt = pl.num_programs(1)
    tt = cur_ref.shape[0]
    xp_ref[0:HALO, :] = jnp.where(t > 0, prev_ref[...], 0.0)
    xp_ref[HALO:HALO + tt, :] = cur_ref[...]
    xp_ref[HALO + tt:HALO + tt + HALO, :] = jnp.where(t < nt - 1, next_ref[...], 0.0)

    span = chunk + 2 * HALO - SUBLANES
    first_tap = HALO - CONV_K // 2

    def one_chunk(base, shift_ref):
        win = xp_ref[pl.ds(base, chunk + 2 * HALO), :]
        for r in range(1, SUBLANES):
            shift_ref[r] = win[r:r + span, :]
        acc = jnp.zeros((chunk, D_CONV), jnp.float32)
        for r in range(SUBLANES):
            for a in range(span // SUBLANES):
                j = SUBLANES * a + r - first_tap
                if 0 <= j < CONV_K and SUBLANES * a + chunk <= span:
                    rows = slice(SUBLANES * a, SUBLANES * a + chunk)
                    tap = win[rows, :] if r == 0 else shift_ref[r, rows, :]
                    acc = acc + tap * w_ref[j:j + 1, :]
        y = acc + b_ref[...]
        mu = jnp.mean(y, axis=-1, keepdims=True)
        yc = y - mu
        yn = yc * lax.rsqrt(jnp.mean(yc * yc, axis=-1, keepdims=True) + EPS) * lg_ref[...] + lb_ref[...]
        z = yn * jax.nn.sigmoid(yn)
        out_ref[pl.ds(base, chunk), :] = _rms(z, og_ref[...]).astype(out_ref.dtype)

    def body(i, carry):
        for half, shift_ref in enumerate(shift_refs):
            one_chunk(pl.multiple_of((2 * i + half) * chunk, chunk), shift_ref)
        return carry

    lax.fori_loop(0, tt // (2 * chunk), body, 0)


def _conv(u, conv_w, conv_b, ln_g, ln_b, og, tt=512, chunk=32):
    b, t, _ = u.shape
    hb = tt // HALO
    n_halo = t // HALO
    fix = lambda bi, ti: (0, 0)
    return pl.pallas_call(
        functools.partial(_conv_kernel, chunk=chunk),
        grid=(b, t // tt),
        in_specs=[
            pl.BlockSpec((None, HALO, D_CONV), lambda bi, ti: (bi, jnp.maximum(ti * hb - 1, 0), 0)),
            pl.BlockSpec((None, tt, D_CONV), lambda bi, ti: (bi, ti, 0)),
            pl.BlockSpec((None, HALO, D_CONV),
                         lambda bi, ti: (bi, jnp.minimum((ti + 1) * hb, n_halo - 1), 0)),
            pl.BlockSpec((CONV_K, D_CONV), fix),
            pl.BlockSpec((1, D_CONV), fix),
            pl.BlockSpec((1, D_CONV), fix),
            pl.BlockSpec((1, D_CONV), fix),
            pl.BlockSpec((1, D_CONV), fix),
        ],
        out_specs=pl.BlockSpec((None, tt, D_CONV), lambda bi, ti: (bi, ti, 0)),
        out_shape=jax.ShapeDtypeStruct((b, t, D_CONV), jnp.bfloat16),
        scratch_shapes=[pltpu.VMEM((tt + 2 * HALO, D_CONV), jnp.float32)]
        + [pltpu.VMEM((SUBLANES, chunk + 2 * HALO - SUBLANES, D_CONV), jnp.float32)] * 2,
        compiler_params=_params(2),
        name="conv",
    )(u, u, u, conv_w, conv_b, ln_g, ln_b, og)


def _natten_kernel(q_ref, kp_ref, kc_ref, kn_ref, vp_ref, vc_ref, vn_ref, bias_ref, og_ref, out_ref):
    lane = lax.broadcasted_iota(jnp.int32, (1, LANES), 1)
    first_head = lane < HEAD_DIM
    outs = []
    for p in range(N_HEADS // 2):
        sl = slice(p * LANES, (p + 1) * LANES)
        q2 = q_ref[:, sl]
        kcat = jnp.concatenate([kp_ref[:, sl], kc_ref[:, sl], kn_ref[:, sl]], axis=0)
        vcat = jnp.concatenate([vp_ref[:, sl], vc_ref[:, sl], vn_ref[:, sl]], axis=0)
        o_pair = []
        for sub in range(2):
            mine = first_head if sub == 0 else jnp.logical_not(first_head)
            qm = jnp.where(mine, q2, jnp.zeros_like(q2))
            s = _dot_nt(qm, kcat) + bias_ref[2 * p + sub]
            m = jnp.max(s, axis=-1, keepdims=True)
            e = jnp.exp(s - m)
            l = jnp.sum(e, axis=-1, keepdims=True)
            o = jnp.dot(e.astype(jnp.bfloat16), vcat, preferred_element_type=jnp.float32)
            o_pair.append(o / l)
        outs.append(jnp.where(first_head, o_pair[0], o_pair[1]))
    y = jnp.concatenate(outs, axis=1)
    out_ref[...] = _rms(y, og_ref[...]).astype(out_ref.dtype)


def _natten(q, k, v, bias, og):
    b, t, _ = q.shape
    tq = Q_ROWS * GRID_W
    nb = t // tq
    prev = lambda bi, i: (bi, jnp.maximum(i - 1, 0), 0)
    cur = lambda bi, i: (bi, i, 0)
    nxt = lambda bi, i: (bi, jnp.minimum(i + 1, nb - 1), 0)
    blk = lambda m: pl.BlockSpec((None, tq, D_ATTN), m)
    variant = lambda bi, i: (jnp.where(i == 0, 0, jnp.where(i == nb - 1, 2, 1)), 0, 0, 0)
    return pl.pallas_call(
        _natten_kernel,
        grid=(b, nb),
        in_specs=[
            blk(cur), blk(prev), blk(cur), blk(nxt), blk(prev), blk(cur), blk(nxt),
            pl.BlockSpec((None, N_HEADS, tq, 3 * tq), variant),
            pl.BlockSpec((1, D_ATTN), lambda bi, i: (0, 0)),
        ],
        out_specs=blk(cur),
        out_shape=jax.ShapeDtypeStruct((b, t, D_ATTN), jnp.bfloat16),
        compiler_params=_params(2),
        name="natten",
    )(q, k, k, k, v, v, v, bias, og)


def _natten_bias(rpb, rows):
    tq = Q_ROWS * GRID_W
    nh, n_dr, n_dc = rpb.shape
    span = 2 * GRID_W - 1
    lo = (GRID_W - 1) - (WIN_COLS - 1)
    padded = jnp.pad(rpb.astype(jnp.float32), ((0, 0), (0, 0), (lo, span - n_dc - lo)))
    flat = jnp.tile(padded, (1, 1, GRID_W))[:, :, GRID_W - 1:GRID_W - 1 + GRID_W * (span - 1)]
    col_t = flat.reshape(nh, n_dr, GRID_W, span - 1)[..., :GRID_W]
    per_a = [col_t[:, WIN_ROWS - 1 - Q_ROWS - a:WIN_ROWS - 1 - Q_ROWS - a + 3 * Q_ROWS]
             for a in range(Q_ROWS)]
    tab = jnp.stack(per_a, axis=1)
    tab = tab.transpose(0, 1, 3, 2, 4).reshape(nh, tq, 3 * tq)

    a = (np.arange(tq) // GRID_W)[:, None]
    c = (np.arange(tq) % GRID_W)[:, None]
    i = (np.arange(3 * tq) // GRID_W)[None, :]
    kc = (np.arange(3 * tq) % GRID_W)[None, :]
    cs = np.clip(c - WIN_COLS // 2, 0, GRID_W - WIN_COLS)
    col_ok = (kc >= cs) & (kc < cs + WIN_COLS)
    tabs = []
    for r0, clamp in ((0, True), (0, False), (rows - Q_ROWS, True)):
        r = r0 + a
        kr = r0 - Q_ROWS + i
        rs = r - WIN_ROWS // 2
        if clamp:
            rs = np.clip(rs, 0, rows - WIN_ROWS)
        ok = col_ok & (kr >= rs) & (kr < rs + WIN_ROWS)
        tabs.append(jnp.where(jnp.asarray(ok)[None], tab, NEG))
    return jnp.stack(tabs, axis=0)


def _out_proj_kernel(x_ref, yc_ref, ya_ref, wo_ref, g_ref, wq_ref, sk_ref, x1_ref, h2_ref, st_ref):
    mix = (jnp.dot(yc_ref[...], wo_ref[0:D_CONV, :], preferred_element_type=jnp.float32)
           + jnp.dot(ya_ref[...], wo_ref[D_CONV:D_CONV + D_ATTN, :], preferred_element_type=jnp.float32))
    x1 = x_ref[...] + mix
    x1_ref[...] = x1
    h2 = _rms(x1, g_ref[...])
    h2_ref[...] = h2
    qry = jnp.dot(h2.astype(jnp.bfloat16), wq_ref[...], preferred_element_type=jnp.float32)
    qry = qry.astype(jnp.bfloat16)
    for j in range(2 * PEER_HEADS):
        st_ref[j] = _dot_nt(sk_ref[j], qry[:, j * PEER_DK_HALF:(j + 1) * PEER_DK_HALF])


def _out_proj(x, yc, ya, w_out, g_ffn, w_query, sub_keys, tm=256):
    n = x.shape[0]
    row = lambda i: (i, 0)
    fix = lambda i: (0, 0)
    nq = w_query.shape[1]
    return pl.pallas_call(
        _out_proj_kernel,
        grid=(n // tm,),
        in_specs=[
            pl.BlockSpec((tm, D_MODEL), row),
            pl.BlockSpec((tm, D_CONV), row),
            pl.BlockSpec((tm, D_ATTN), row),
            pl.BlockSpec((D_CONV + D_ATTN, D_MODEL), fix),
            pl.BlockSpec((1, D_MODEL), fix),
            pl.BlockSpec((D_MODEL, nq), fix),
            pl.BlockSpec((2 * PEER_HEADS, PEER_KEYS, PEER_DK_HALF), lambda i: (0, 0, 0)),
        ],
        out_specs=[
            pl.BlockSpec((tm, D_MODEL), row),
            pl.BlockSpec((tm, D_MODEL), row),
            pl.BlockSpec((2 * PEER_HEADS, PEER_KEYS, tm), lambda i: (0, 0, i)),
        ],
        out_shape=[
            jax.ShapeDtypeStruct((n, D_MODEL), jnp.float32),
            jax.ShapeDtypeStruct((n, D_MODEL), jnp.float32),
            jax.ShapeDtypeStruct((2 * PEER_HEADS, PEER_KEYS, n), jnp.float32),
        ],
        compiler_params=_params(),
        name="out_proj",
    )(x, yc, ya, w_out, g_ffn, w_query, sub_keys)


BIG_ID = 2.0 ** 30


def _top_rows(s, row_id, slot, count):
    vals = jnp.zeros(slot.shape, jnp.float32)
    ids = jnp.zeros(slot.shape, jnp.float32)
    for r in range(count):
        m = jnp.max(s, axis=0, keepdims=True)
        sel = jnp.min(jnp.where(s == m, row_id, BIG_ID), axis=0, keepdims=True)
        vals = jnp.where(slot == r, m, vals)
        ids = jnp.where(slot == r, sel, ids)
        s = jnp.where(row_id == sel, -jnp.inf, s)
    return vals, ids


def _route_kernel(st_ref, idx_ref, gate_ref):
    for t0 in range(0, st_ref.shape[2], LANES):
        _route_tile(st_ref, idx_ref, gate_ref, t0)


def _route_tile(st_ref, idx_ref, gate_ref, t0):
    tn = LANES
    half = PEER_TOPK // 2
    key_id = lax.broadcasted_iota(jnp.int32, (PEER_KEYS, tn), 0).astype(jnp.float32)
    slot = lax.broadcasted_iota(jnp.int32, (PEER_TOPK, tn), 0)
    sub16 = slot.astype(jnp.float32)
    sub8 = sub16[0:half]
    row_scale0 = float(PEER_KEYS * ROW_PLANES)
    row_scale1 = float(ROW_PLANES)
    e_all, g_all = [], []
    for h in range(PEER_HEADS):
        v0, i0 = _top_rows(st_ref[2 * h, :, t0:t0 + tn], key_id, slot, PEER_TOPK)
        v1, i1 = _top_rows(st_ref[2 * h + 1, :, t0:t0 + tn], key_id, slot, PEER_TOPK)
        cand, row, flat = [], [], []
        for i in range(2):
            rows = PEER_TOPK // (i + 1)
            cand.append(v0[i:i + 1] + v1[0:rows])
            row.append(i0[i:i + 1] * row_scale0 + i1[0:rows] * row_scale1)
            flat.append(sub16[0:rows] + float(i * PEER_TOPK))
        for pieces in (((2, 5), (4, 3)), ((3, 4), (5, 2), (6, 2)), ((7, 2),)):
            c = jnp.full((half, tn), -jnp.inf, jnp.float32)
            rw = jnp.zeros((half, tn), jnp.float32)
            fl = jnp.full((half, tn), BIG_ID, jnp.float32)
            start = 0
            for i, nj in pieces:
                assert nj == PEER_TOPK // (i + 1)
                v1s = v1[0:half] if start == 0 else pltpu.roll(v1[0:half], start, axis=0)
                i1s = i1[0:half] if start == 0 else pltpu.roll(i1[0:half], start, axis=0)
                here = (sub8 >= start) & (sub8 < start + nj)
                c = jnp.where(here, v0[i:i + 1] + v1s, c)
                rw = jnp.where(here, i0[i:i + 1] * row_scale0 + i1s * row_scale1, rw)
                fl = jnp.where(here, sub8 + float(i * PEER_TOPK - start), fl)
                start += nj
            cand.append(c)
            row.append(rw)
            flat.append(fl)
        cand.append(v0[half:] + v1[0:1])
        row.append(i0[half:] * row_scale0 + i1[0:1] * row_scale1)
        flat.append((sub8 + float(half)) * float(PEER_TOPK))
        cand = jnp.concatenate(cand, axis=0)
        row = jnp.concatenate(row, axis=0)
        flat = jnp.concatenate(flat, axis=0)
        cv = jnp.zeros((PEER_TOPK, tn), jnp.float32)
        picked = jnp.zeros((PEER_TOPK, tn), jnp.float32)
        for r in range(PEER_TOPK):
            m = jnp.max(cand, axis=0, keepdims=True)
            fsel = jnp.min(jnp.where(cand == m, flat, BIG_ID), axis=0, keepdims=True)
            hit = flat == fsel
            got = jnp.sum(jnp.where(hit, row, 0.0), axis=0, keepdims=True)
            cv = jnp.where(slot == r, m, cv)
            picked = jnp.where(slot == r, got, picked)
            cand = jnp.where(hit, -jnp.inf, cand)
        ex = jnp.exp(cv - cv[0:1])
        g_all.append(ex / jnp.sum(ex, axis=0, keepdims=True))
        e_all.append(picked)
    e_all = jnp.concatenate(e_all, axis=0)
    g_all = jnp.concatenate(g_all, axis=0)
    idx_ref[t0:t0 + tn, :] = e_all.T.astype(jnp.int32)
    gate_ref[t0:t0 + tn, :] = g_all.T


def _route(st, tn=256):
    n = st.shape[2]
    return pl.pallas_call(
        _route_kernel,
        grid=(n // tn,),
        in_specs=[pl.BlockSpec((2 * PEER_HEADS, PEER_KEYS, tn), lambda i: (0, 0, i))],
        out_specs=[pl.BlockSpec((tn, N_SEL), lambda i: (i, 0))] * 2,
        out_shape=[jax.ShapeDtypeStruct((n, N_SEL), jnp.int32),
                   jax.ShapeDtypeStruct((n, N_SEL), jnp.float32)],
        compiler_params=_params(),
        name="route",
    )(st)


def _load_table(tab_hbm, tab_vmem, sem):
    @pl.when(pl.program_id(0) == 0)
    def _():
        cp = pltpu.make_async_copy(tab_hbm, tab_vmem, sem)
        cp.start()
        cp.wait()


N_OFF = 8


def _gather_rows(row_ref, tab_ref, stage_ref, offs):
    for j in range(N_SEL // N_OFF):
        sub = row_ref.at[pl.ds(j * N_OFF, N_OFF)]
        for i in range(N_OFF):
            k = j * N_OFF + i
            r = pl.multiple_of(sub[offs[i]], ROW_PLANES)
            stage_ref[pl.ds(k, ROW_PLANES, stride=PLANE_ROWS), :] = tab_ref[pl.ds(r, ROW_PLANES), :]


def _gather_rows_rolled(row_ref, tab_ref, stage_ref):
    def body(j, carry):
        for i in range(N_OFF):
            k = j * N_OFF + i
            r = pl.multiple_of(row_ref[k], ROW_PLANES)
            stage_ref[pl.ds(k, ROW_PLANES, stride=PLANE_ROWS), :] = tab_ref[pl.ds(r, ROW_PLANES), :]
        return carry

    lax.fori_loop(0, N_SEL // N_OFF, body, 0)


def _staged_matrix(stage_ref):
    los, his = [], []
    for s in range(ROW_PLANES):
        p = stage_ref[pl.ds(PLANE_ROWS * s, N_SEL), :]
        los.append(pltpu.bitcast(p << 16, jnp.float32))
        his.append(pltpu.bitcast(p & jnp.int32(-65536), jnp.float32))
    return jnp.concatenate(los + his, axis=1).astype(jnp.bfloat16)


def _split_rows(row):
    hi = row.astype(jnp.bfloat16)
    lo = (row - hi.astype(jnp.float32)).astype(jnp.bfloat16)
    return jnp.concatenate([hi, lo, jnp.zeros((6, row.shape[1]), jnp.bfloat16)], axis=0)


def _token_pipeline(n_tok, idx_ref, tab_ref, stages, issue, finish, placeholder):
    g = len(stages)
    zero = jnp.minimum(pl.program_id(0), 0)
    offs = [zero + i for i in range(N_OFF)]
    for s in range(g):
        _gather_rows_rolled(idx_ref.at[s], tab_ref, stages[s])

    def group(i, prev):
        c = g * i
        for s in range(g):
            finish(jnp.maximum(c + s - g, 0), s, prev[s])
        cur = tuple(issue(c + s, s) for s in range(g))
        for s in range(g):
            _gather_rows(idx_ref.at[jnp.minimum(c + s + g, n_tok - 1)], tab_ref, stages[s], offs)
        return cur

    last = lax.fori_loop(0, n_tok // g, group, (placeholder,) * g)
    for s in range(g):
        finish(n_tok - g + s, s, last[s])


def _peer_u_kernel(idx_ref, h_ref, gate_ref, ut_hbm, w_ref, ut_vmem, *rest):
    stages, sem = rest[:-1], rest[-1]
    _load_table(ut_hbm, ut_vmem, sem)

    def issue(c, slot):
        return _dot_nt(_split_rows(h_ref[pl.ds(c, 1), :]), _staged_matrix(stages[slot]))

    def finish(c, slot, a8):
        a = a8[0:1] + a8[1:2]
        act = 0.5 * a * (1.0 + lax.erf(a * np.float32(1.0 / np.sqrt(2.0))))
        w_ref[pl.ds(c, 1), :] = gate_ref[pl.ds(c, 1), :] * act

    _token_pipeline(h_ref.shape[0], idx_ref, ut_vmem, stages, issue, finish,
                    jnp.zeros((8, N_SEL), jnp.float32))


def _peer_v_kernel(idx_ref, w_ref, x_ref, vt_hbm, out_ref, vt_vmem, *rest):
    stages, sem = rest[:-1], rest[-1]
    _load_table(vt_hbm, vt_vmem, sem)

    def issue(c, slot):
        w_col = jnp.broadcast_to(w_ref[pl.ds(c, 1), :], (N_SEL, N_SEL)).T
        los, his = [], []
        for s in range(ROW_PLANES):
            p = stages[slot][pl.ds(PLANE_ROWS * s, N_SEL), :]
            lo = pltpu.bitcast(p << 16, jnp.float32)
            hi = pltpu.bitcast(p & jnp.int32(-65536), jnp.float32)
            los.append(jnp.sum(lo * w_col, axis=0, keepdims=True))
            his.append(jnp.sum(hi * w_col, axis=0, keepdims=True))
        return jnp.concatenate(los + his, axis=1)

    def finish(c, slot, peer):
        out_ref[pl.ds(c, 1), :] = x_ref[pl.ds(c, 1), :] + peer

    _token_pipeline(x_ref.shape[0], idx_ref, vt_vmem, stages, issue, finish,
                    jnp.zeros((1, D_MODEL), jnp.float32))


def _peer_call(kernel_fn, name, idx, a, b, table, out_cols, group, tm=512):
    n = idx.shape[0]
    row = lambda i: (i, 0)
    return pl.pallas_call(
        kernel_fn,
        grid=(n // tm,),
        in_specs=[
            pl.BlockSpec((tm, N_SEL), row, memory_space=pltpu.SMEM),
            pl.BlockSpec((tm, a.shape[1]), row),
            pl.BlockSpec((tm, b.shape[1]), row),
            pl.BlockSpec(memory_space=pl.ANY),
        ],
        out_specs=pl.BlockSpec((tm, out_cols), row),
        out_shape=jax.ShapeDtypeStruct((n, out_cols), jnp.float32),
        scratch_shapes=[pltpu.VMEM((N_EXPERTS * ROW_PLANES, LANES), jnp.int32)]
        + [pltpu.VMEM((ROW_PLANES * PLANE_ROWS, LANES), jnp.int32)] * group
        + [pltpu.SemaphoreType.DMA],
        compiler_params=_params(),
        name=name,
    )(idx, a, b, table)


def _pack_table(t):
    tb = t.astype(jnp.bfloat16)
    lo = lax.bitcast_convert_type(tb[:, :ROW_WORDS], jnp.uint16).astype(jnp.uint32)
    hi = lax.bitcast_convert_type(tb[:, ROW_WORDS:], jnp.uint16).astype(jnp.uint32)
    words = lax.bitcast_convert_type(lo | (hi << 16), jnp.int32)
    return words.reshape(t.shape[0] * ROW_PLANES, LANES)


def _layer(x, p):
    b, t, _ = x.shape
    n = b * t
    xf = x.reshape(n, D_MODEL)
    u, q, k, v = _in_proj(xf, p["g_mix"], p["w_in"], p["hsum"], p["qg"], p["kg"])
    seq = lambda z: z.reshape(b, t, z.shape[-1])
    yc = _conv(seq(u), p["conv_w"], p["conv_b"], p["conv_ln_g"], p["conv_ln_b"], p["g_out_conv"])
    bias = _natten_bias(p["rpb"], t // GRID_W)
    ya = _natten(seq(q), seq(k), seq(v), bias, p["g_out_attn"])
    x1, h2, st = _out_proj(xf, yc.reshape(n, D_CONV), ya.reshape(n, D_ATTN), p["w_out"], p["g_ffn"],
                           p["w_query"], p["sub_keys"])
    idx, gate = _route(st)
    w = _peer_call(_peer_u_kernel, "peer_u", idx, h2, gate, p["ut"], N_SEL, group=8)
    y = _peer_call(_peer_v_kernel, "peer_v", idx, w, x1, p["vt"], D_MODEL, group=8)
    return y.reshape(b, t, D_MODEL)


def kernel(x_prompt, x_sample, g_mix, w_in, conv_w, conv_b, conv_ln_g, conv_ln_b, q_norm_g, k_norm_g, rpb, g_out_conv, g_out_attn, w_out, g_ffn, w_query, sub_keys, expert_u, expert_v):
    depth = g_mix.shape[0]
    head = np.arange(D_ATTN) // HEAD_DIM
    hsum = jnp.asarray((head[:, None] == head[None, :]) / HEAD_DIM, jnp.bfloat16)
    scale = HEAD_DIM ** -0.5
    y_prompt, y_sample = x_prompt, x_sample
    for l in range(depth):
        row = lambda z: z[l].reshape(1, -1)
        p = dict(
            g_mix=row(g_mix), w_in=w_in[l].astype(jnp.bfloat16), hsum=hsum,
            qg=jnp.tile(q_norm_g[l], N_HEADS).reshape(1, -1) * scale,
            kg=jnp.tile(k_norm_g[l], N_HEADS).reshape(1, -1),
            conv_w=conv_w[l], conv_b=row(conv_b), conv_ln_g=row(conv_ln_g), conv_ln_b=row(conv_ln_b),
            g_out_conv=row(g_out_conv), g_out_attn=row(g_out_attn), rpb=rpb[l],
            w_out=w_out[l].astype(jnp.bfloat16), g_ffn=row(g_ffn),
            w_query=w_query[l].astype(jnp.bfloat16),
            sub_keys=sub_keys[l].reshape(2 * PEER_HEADS, PEER_KEYS, PEER_DK_HALF).astype(jnp.bfloat16),
            ut=_pack_table(expert_u[l]), vt=_pack_table(expert_v[l]),
        )
        y_prompt = _layer(y_prompt, p)
        y_sample = _layer(y_sample, p)
    return (y_prompt, y_sample)
```

```python
import functools

import numpy as np
import jax
import jax.numpy as jnp
from jax import lax
from jax.experimental import pallas as pl
from jax.experimental.pallas import tpu as pltpu

EPS = 1e-6
NEG = -1e30

D_MODEL = 1024
D_CONV = 512
D_ATTN = 512
N_HEADS = 8
HEAD_DIM = 64
CONV_K = 31
GRID_W = 64
WIN_ROWS = 8
WIN_COLS = 16

PEER_HEADS = 8
PEER_KEYS = 128
PEER_DK_HALF = 128
PEER_TOPK = 16
N_SEL = PEER_HEADS * PEER_TOPK
N_EXPERTS = PEER_KEYS * PEER_KEYS

LANES = 128
SUBLANES = 8
ROW_WORDS = D_MODEL // 2
ROW_PLANES = ROW_WORDS // LANES
PLANE_ROWS = N_SEL + 8

Q_ROWS = 4
HALO = 16

VMEM_LIMIT = 56 * 1024 * 1024


def _params(n_axes=1, flags=None):
    return pltpu.CompilerParams(
        dimension_semantics=("arbitrary",) * n_axes, vmem_limit_bytes=VMEM_LIMIT, flags=flags)


def _rms(x, g):
    return x * lax.rsqrt(jnp.mean(x * x, axis=-1, keepdims=True) + EPS) * g


def _dot_nt(a, b):
    return lax.dot_general(a, b, (((1,), (1,)), ((), ())), preferred_element_type=jnp.float32)


def _in_proj_kernel(x_ref, g_ref, w_ref, hsum_ref, qg_ref, kg_ref, u_ref, q_ref, k_ref, v_ref):
    h = _rms(x_ref[...], g_ref[...])
    proj = jnp.dot(h.astype(jnp.bfloat16), w_ref[...], preferred_element_type=jnp.float32)
    a = proj[:, 0:D_CONV]
    gate = proj[:, D_CONV:2 * D_CONV]
    u_ref[...] = a * jax.nn.sigmoid(gate)

    def head_rms(t, gain):
        tt = t * t
        hi = tt.astype(jnp.bfloat16)
        lo = (tt - hi.astype(jnp.float32)).astype(jnp.bfloat16)
        ms = (jnp.dot(hi, hsum_ref[...], preferred_element_type=jnp.float32)
              + jnp.dot(lo, hsum_ref[...], preferred_element_type=jnp.float32))
        return t * lax.rsqrt(ms + EPS) * gain

    o = 2 * D_CONV
    q_ref[...] = head_rms(proj[:, o:o + D_ATTN], qg_ref[...]).astype(jnp.bfloat16)
    k_ref[...] = head_rms(proj[:, o + D_ATTN:o + 2 * D_ATTN], kg_ref[...]).astype(jnp.bfloat16)
    v_ref[...] = proj[:, o + 2 * D_ATTN:o + 3 * D_ATTN].astype(jnp.bfloat16)


def _in_proj(x, g_mix, w_in, hsum, qg, kg, tm=512):
    n = x.shape[0]
    d_in = w_in.shape[1]
    row = lambda i: (i, 0)
    fix = lambda i: (0, 0)
    return pl.pallas_call(
        _in_proj_kernel,
        grid=(n // tm,),
        in_specs=[
            pl.BlockSpec((tm, D_MODEL), row),
            pl.BlockSpec((1, D_MODEL), fix),
            pl.BlockSpec((D_MODEL, d_in), fix),
            pl.BlockSpec((D_ATTN, D_ATTN), fix),
            pl.BlockSpec((1, D_ATTN), fix),
            pl.BlockSpec((1, D_ATTN), fix),
        ],
        out_specs=[pl.BlockSpec((tm, D_CONV), row)] + [pl.BlockSpec((tm, D_ATTN), row)] * 3,
        out_shape=[jax.ShapeDtypeStruct((n, D_CONV), jnp.float32)]
        + [jax.ShapeDtypeStruct((n, D_ATTN), jnp.bfloat16)] * 3,
        compiler_params=_params(),
        name="in_proj",
    )(x, g_mix, w_in, hsum, qg, kg)


def _conv_kernel(prev_ref, cur_ref, next_ref, w_ref, b_ref, lg_ref, lb_ref, og_ref, out_ref, xp_ref,
                 *shift_refs, chunk):
    t = pl.program_id(1)
    nt = pl.num_programs(1)
    tt = cur_ref.shape[0]
    xp_ref[0:HALO, :] = jnp.where(t > 0, prev_ref[...], 0.0)
    xp_ref[HALO:HALO + tt, :] = cur_ref[...]
    xp_ref[HALO + tt:HALO + tt + HALO, :] = jnp.where(t < nt - 1, next_ref[...], 0.0)

    span = chunk + 2 * HALO - SUBLANES
    first_tap = HALO - CONV_K // 2

    def one_chunk(base, shift_ref):
        win = xp_ref[pl.ds(base, chunk + 2 * HALO), :]
        for r in range(1, SUBLANES):
            shift_ref[r] = win[r:r + span, :]
        acc = jnp.zeros((chunk, D_CONV), jnp.float32)
        for r in range(SUBLANES):
            for a in range(span // SUBLANES):
                j = SUBLANES * a + r - first_tap
                if 0 <= j < CONV_K and SUBLANES * a + chunk <= span:
                    rows = slice(SUBLANES * a, SUBLANES * a + chunk)
                    tap = win[rows, :] if r == 0 else shift_ref[r, rows, :]
                    acc = acc + tap * w_ref[j:j + 1, :]
        y = acc + b_ref[...]
        mu = jnp.mean(y, axis=-1, keepdims=True)
        yc = y - mu
        yn = yc * lax.rsqrt(jnp.mean(yc * yc, axis=-1, keepdims=True) + EPS) * lg_ref[...] + lb_ref[...]
        z = yn * jax.nn.sigmoid(yn)
        out_ref[pl.ds(base, chunk), :] = _rms(z, og_ref[...]).astype(out_ref.dtype)

    def body(i, carry):
        for half, shift_ref in enumerate(shift_refs):
            one_chunk(pl.multiple_of((2 * i + half) * chunk, chunk), shift_ref)
        return carry

    lax.fori_loop(0, tt // (2 * chunk), body, 0)


def _conv(u, conv_w, conv_b, ln_g, ln_b, og, tt=512, chunk=32):
    b, t, _ = u.shape
    hb = tt // HALO
    n_halo = t // HALO
    fix = lambda bi, ti: (0, 0)
    return pl.pallas_call(
        functools.partial(_conv_kernel, chunk=chunk),
        grid=(b, t // tt),
        in_specs=[
            pl.BlockSpec((None, HALO, D_CONV), lambda bi, ti: (bi, jnp.maximum(ti * hb - 1, 0), 0)),
            pl.BlockSpec((None, tt, D_CONV), lambda bi, ti: (bi, ti, 0)),
            pl.BlockSpec((None, HALO, D_CONV),
                         lambda bi, ti: (bi, jnp.minimum((ti + 1) * hb, n_halo - 1), 0)),
            pl.BlockSpec((CONV_K, D_CONV), fix),
            pl.BlockSpec((1, D_CONV), fix),
            pl.BlockSpec((1, D_CONV), fix),
            pl.BlockSpec((1, D_CONV), fix),
            pl.BlockSpec((1, D_CONV), fix),
        ],
        out_specs=pl.BlockSpec((None, tt, D_CONV), lambda bi, ti: (bi, ti, 0)),
        out_shape=jax.ShapeDtypeStruct((b, t, D_CONV), jnp.bfloat16),
        scratch_shapes=[pltpu.VMEM((tt + 2 * HALO, D_CONV), jnp.float32)]
        + [pltpu.VMEM((SUBLANES, chunk + 2 * HALO - SUBLANES, D_CONV), jnp.float32)] * 2,
        compiler_params=_params(2),
        name="conv",
    )(u, u, u, conv_w, conv_b, ln_g, ln_b, og)


def _natten_kernel(q_ref, kp_ref, kc_ref, kn_ref, vp_ref, vc_ref, vn_ref, bias_ref, og_ref, out_ref):
    lane = lax.broadcasted_iota(jnp.int32, (1, LANES), 1)
    first_head = lane < HEAD_DIM
    outs = []
    for p in range(N_HEADS // 2):
        sl = slice(p * LANES, (p + 1) * LANES)
        q2 = q_ref[:, sl]
        kcat = jnp.concatenate([kp_ref[:, sl], kc_ref[:, sl], kn_ref[:, sl]], axis=0)
        vcat = jnp.concatenate([vp_ref[:, sl], vc_ref[:, sl], vn_ref[:, sl]], axis=0)
        o_pair = []
        for sub in range(2):
            mine = first_head if sub == 0 else jnp.logical_not(first_head)
            qm = jnp.where(mine, q2, jnp.zeros_like(q2))
            s = _dot_nt(qm, kcat) + bias_ref[2 * p + sub]
            m = jnp.max(s, axis=-1, keepdims=True)
            e = jnp.exp(s - m)
            l = jnp.sum(e, axis=-1, keepdims=True)
            o = jnp.dot(e.astype(jnp.bfloat16), vcat, preferred_element_type=jnp.float32)
            o_pair.append(o / l)
        outs.append(jnp.where(first_head, o_pair[0], o_pair[1]))
    y = jnp.concatenate(outs, axis=1)
    out_ref[...] = _rms(y, og_ref[...]).astype(out_ref.dtype)


def _natten(q, k, v, bias, og):
    b, t, _ = q.shape
    tq = Q_ROWS * GRID_W
    nb = t // tq
    prev = lambda bi, i: (bi, jnp.maximum(i - 1, 0), 0)
    cur = lambda bi, i: (bi, i, 0)
    nxt = lambda bi, i: (bi, jnp.minimum(i + 1, nb - 1), 0)
    blk = lambda m: pl.BlockSpec((None, tq, D_ATTN), m)
    variant = lambda bi, i: (jnp.where(i == 0, 0, jnp.where(i == nb - 1, 2, 1)), 0, 0, 0)
    return pl.pallas_call(
        _natten_kernel,
        grid=(b, nb),
        in_specs=[
            blk(cur), blk(prev), blk(cur), blk(nxt), blk(prev), blk(cur), blk(nxt),
            pl.BlockSpec((None, N_HEADS, tq, 3 * tq), variant),
            pl.BlockSpec((1, D_ATTN), lambda bi, i: (0, 0)),
        ],
        out_specs=blk(cur),
        out_shape=jax.ShapeDtypeStruct((b, t, D_ATTN), jnp.bfloat16),
        compiler_params=_params(2),
        name="natten",
    )(q, k, k, k, v, v, v, bias, og)


def _natten_bias(rpb, rows):
    tq = Q_ROWS * GRID_W
    nh, n_dr, n_dc = rpb.shape
    span = 2 * GRID_W - 1
    lo = (GRID_W - 1) - (WIN_COLS - 1)
    padded = jnp.pad(rpb.astype(jnp.float32), ((0, 0), (0, 0), (lo, span - n_dc - lo)))
    flat = jnp.tile(padded, (1, 1, GRID_W))[:, :, GRID_W - 1:GRID_W - 1 + GRID_W * (span - 1)]
    col_t = flat.reshape(nh, n_dr, GRID_W, span - 1)[..., :GRID_W]
    per_a = [col_t[:, WIN_ROWS - 1 - Q_ROWS - a:WIN_ROWS - 1 - Q_ROWS - a + 3 * Q_ROWS]
             for a in range(Q_ROWS)]
    tab = jnp.stack(per_a, axis=1)
    tab = tab.transpose(0, 1, 3, 2, 4).reshape(nh, tq, 3 * tq)

    a = (np.arange(tq) // GRID_W)[:, None]
    c = (np.arange(tq) % GRID_W)[:, None]
    i = (np.arange(3 * tq) // GRID_W)[None, :]
    kc = (np.arange(3 * tq) % GRID_W)[None, :]
    cs = np.clip(c - WIN_COLS // 2, 0, GRID_W - WIN_COLS)
    col_ok = (kc >= cs) & (kc < cs + WIN_COLS)
    tabs = []
    for r0, clamp in ((0, True), (0, False), (rows - Q_ROWS, True)):
        r = r0 + a
        kr = r0 - Q_ROWS + i
        rs = r - WIN_ROWS // 2
        if clamp:
            rs = np.clip(rs, 0, rows - WIN_ROWS)
        ok = col_ok & (kr >= rs) & (kr < rs + WIN_ROWS)
        tabs.append(jnp.where(jnp.asarray(ok)[None], tab, NEG))
    return jnp.stack(tabs, axis=0)


def _out_proj_kernel(x_ref, yc_ref, ya_ref, wo_ref, g_ref, wq_ref, sk_ref, x1_ref, h2_ref, idx_ref,
                     gate_ref, st_ref):
    mix = (jnp.dot(yc_ref[...], wo_ref[0:D_CONV, :], preferred_element_type=jnp.float32)
           + jnp.dot(ya_ref[...], wo_ref[D_CONV:D_CONV + D_ATTN, :], preferred_element_type=jnp.float32))
    x1 = x_ref[...] + mix
    x1_ref[...] = x1
    h2 = _rms(x1, g_ref[...])
    h2_ref[...] = h2
    qry = jnp.dot(h2.astype(jnp.bfloat16), wq_ref[...], preferred_element_type=jnp.float32)
    qry = qry.astype(jnp.bfloat16)
    for j in range(2 * PEER_HEADS):
        st_ref[j] = _dot_nt(sk_ref[j], qry[:, j * PEER_DK_HALF:(j + 1) * PEER_DK_HALF])
    for t0 in range(0, st_ref.shape[2], LANES):
        _route_tile(st_ref, idx_ref, gate_ref, t0)


def _out_proj(x, yc, ya, w_out, g_ffn, w_query, sub_keys, tm=256):
    n = x.shape[0]
    row = lambda i: (i, 0)
    fix = lambda i: (0, 0)
    nq = w_query.shape[1]
    return pl.pallas_call(
        _out_proj_kernel,
        grid=(n // tm,),
        in_specs=[
            pl.BlockSpec((tm, D_MODEL), row),
            pl.BlockSpec((tm, D_CONV), row),
            pl.BlockSpec((tm, D_ATTN), row),
            pl.BlockSpec((D_CONV + D_ATTN, D_MODEL), fix),
            pl.BlockSpec((1, D_MODEL), fix),
            pl.BlockSpec((D_MODEL, nq), fix),
            pl.BlockSpec((2 * PEER_HEADS, PEER_KEYS, PEER_DK_HALF), lambda i: (0, 0, 0)),
        ],
        out_specs=[
            pl.BlockSpec((tm, D_MODEL), row),
            pl.BlockSpec((tm, D_MODEL), row),
            pl.BlockSpec((tm, N_SEL), row),
            pl.BlockSpec((tm, N_SEL), row),
        ],
        out_shape=[
            jax.ShapeDtypeStruct((n, D_MODEL), jnp.float32),
            jax.ShapeDtypeStruct((n, D_MODEL), jnp.float32),
            jax.ShapeDtypeStruct((n, N_SEL), jnp.int32),
            jax.ShapeDtypeStruct((n, N_SEL), jnp.float32),
        ],
        scratch_shapes=[pltpu.VMEM((2 * PEER_HEADS, PEER_KEYS, tm), jnp.float32)],
        compiler_params=_params(),
        name="out_proj",
    )(x, yc, ya, w_out, g_ffn, w_query, sub_keys)


BIG_ID = 2.0 ** 30


def _top_rows(s, row_id, slot, count):
    vals = jnp.zeros(slot.shape, jnp.float32)
    ids = jnp.zeros(slot.shape, jnp.float32)
    for r in range(count):
        m = jnp.max(s, axis=0, keepdims=True)
        sel = jnp.min(jnp.where(s == m, row_id, BIG_ID), axis=0, keepdims=True)
        vals = jnp.where(slot == r, m, vals)
        ids = jnp.where(slot == r, sel, ids)
        s = jnp.where(row_id == sel, -jnp.inf, s)
    return vals, ids


def _route_tile(st_ref, idx_ref, gate_ref, t0):
    tn = LANES
    half = PEER_TOPK // 2
    key_id = lax.broadcasted_iota(jnp.int32, (PEER_KEYS, tn), 0).astype(jnp.float32)
    slot = lax.broadcasted_iota(jnp.int32, (PEER_TOPK, tn), 0)
    sub16 = slot.astype(jnp.float32)
    sub8 = sub16[0:half]
    row_scale0 = float(PEER_KEYS * ROW_PLANES)
    row_scale1 = float(ROW_PLANES)
    e_all, g_all = [], []
    for h in range(PEER_HEADS):
        v0, i0 = _top_rows(st_ref[2 * h, :, t0:t0 + tn], key_id, slot, PEER_TOPK)
        v1, i1 = _top_rows(st_ref[2 * h + 1, :, t0:t0 + tn], key_id, slot, PEER_TOPK)
        cand, row, flat = [], [], []
        for i in range(2):
            rows = PEER_TOPK // (i + 1)
            cand.append(v0[i:i + 1] + v1[0:rows])
            row.append(i0[i:i + 1] * row_scale0 + i1[0:rows] * row_scale1)
            flat.append(sub16[0:rows] + float(i * PEER_TOPK))
        for pieces in (((2, 5), (4, 3)), ((3, 4), (5, 2), (6, 2)), ((7, 2),)):
            c = jnp.full((half, tn), -jnp.inf, jnp.float32)
            rw = jnp.zeros((half, tn), jnp.float32)
            fl = jnp.full((half, tn), BIG_ID, jnp.float32)
            start = 0
            for i, nj in pieces:
                assert nj == PEER_TOPK // (i + 1)
                v1s = v1[0:half] if start == 0 else pltpu.roll(v1[0:half], start, axis=0)
                i1s = i1[0:half] if start == 0 else pltpu.roll(i1[0:half], start, axis=0)
                here = (sub8 >= start) & (sub8 < start + nj)
                c = jnp.where(here, v0[i:i + 1] + v1s, c)
                rw = jnp.where(here, i0[i:i + 1] * row_scale0 + i1s * row_scale1, rw)
                fl = jnp.where(here, sub8 + float(i * PEER_TOPK - start), fl)
                start += nj
            cand.append(c)
            row.append(rw)
            flat.append(fl)
        cand.append(v0[half:] + v1[0:1])
        row.append(i0[half:] * row_scale0 + i1[0:1] * row_scale1)
        flat.append((sub8 + float(half)) * float(PEER_TOPK))
        cand = jnp.concatenate(cand, axis=0)
        row = jnp.concatenate(row, axis=0)
        flat = jnp.concatenate(flat, axis=0)
        cv = jnp.zeros((PEER_TOPK, tn), jnp.float32)
        picked = jnp.zeros((PEER_TOPK, tn), jnp.float32)
        for r in range(PEER_TOPK):
            m = jnp.max(cand, axis=0, keepdims=True)
            fsel = jnp.min(jnp.where(cand == m, flat, BIG_ID), axis=0, keepdims=True)
            hit = flat == fsel
            got = jnp.sum(jnp.where(hit, row, 0.0), axis=0, keepdims=True)
            cv = jnp.where(slot == r, m, cv)
            picked = jnp.where(slot == r, got, picked)
            cand = jnp.where(hit, -jnp.inf, cand)
        ex = jnp.exp(cv - cv[0:1])
        g_all.append(ex / jnp.sum(ex, axis=0, keepdims=True))
        e_all.append(picked)
    e_all = jnp.concatenate(e_all, axis=0)
    g_all = jnp.concatenate(g_all, axis=0)
    idx_ref[t0:t0 + tn, :] = e_all.T.astype(jnp.int32)
    gate_ref[t0:t0 + tn, :] = g_all.T


def _load_table(tab_hbm, tab_vmem, sem):
    @pl.when(pl.program_id(0) == 0)
    def _():
        cp = pltpu.make_async_copy(tab_hbm, tab_vmem, sem)
        cp.start()
        cp.wait()


N_OFF = 8


def _gather_rows(row_ref, tab_ref, stage_ref, offs):
    for j in range(N_SEL // N_OFF):
        sub = row_ref.at[pl.ds(j * N_OFF, N_OFF)]
        for i in range(N_OFF):
            k = j * N_OFF + i
            r = pl.multiple_of(sub[offs[i]], ROW_PLANES)
            stage_ref[pl.ds(k, ROW_PLANES, stride=PLANE_ROWS), :] = tab_ref[pl.ds(r, ROW_PLANES), :]


def _gather_rows_rolled(row_ref, tab_ref, stage_ref):
    def body(j, carry):
        for i in range(N_OFF):
            k = j * N_OFF + i
            r = pl.multiple_of(row_ref[k], ROW_PLANES)
            stage_ref[pl.ds(k, ROW_PLANES, stride=PLANE_ROWS), :] = tab_ref[pl.ds(r, ROW_PLANES), :]
        return carry

    lax.fori_loop(0, N_SEL // N_OFF, body, 0)


def _staged_matrix(stage_ref):
    los, his = [], []
    for s in range(ROW_PLANES):
        p = stage_ref[pl.ds(PLANE_ROWS * s, N_SEL), :]
        los.append(pltpu.bitcast(p << 16, jnp.float32))
        his.append(pltpu.bitcast(p & jnp.int32(-65536), jnp.float32))
    return jnp.concatenate(los + his, axis=1).astype(jnp.bfloat16)


def _split_rows(row):
    hi = row.astype(jnp.bfloat16)
    lo = (row - hi.astype(jnp.float32)).astype(jnp.bfloat16)
    return jnp.concatenate([hi, lo, jnp.zeros((6, row.shape[1]), jnp.bfloat16)], axis=0)


def _token_pipeline(n_tok, idx_ref, tab_ref, stages, issue, finish, placeholder):
    g = len(stages)
    zero = jnp.minimum(pl.program_id(0), 0)
    offs = [zero + i for i in range(N_OFF)]
    for s in range(g):
        _gather_rows_rolled(idx_ref.at[s], tab_ref, stages[s])

    def group(i, prev):
        c = g * i
        for s in range(g):
            finish(jnp.maximum(c + s - g, 0), s, prev[s])
        cur = tuple(issue(c + s, s) for s in range(g))
        for s in range(g):
            _gather_rows(idx_ref.at[jnp.minimum(c + s + g, n_tok - 1)], tab_ref, stages[s], offs)
        return cur

    last = lax.fori_loop(0, n_tok // g, group, (placeholder,) * g)
    for s in range(g):
        finish(n_tok - g + s, s, last[s])


def _peer_u_kernel(idx_ref, h_ref, gate_ref, ut_hbm, w_ref, ut_vmem, *rest):
    stages, sem = rest[:-1], rest[-1]
    _load_table(ut_hbm, ut_vmem, sem)

    def issue(c, slot):
        return _dot_nt(_split_rows(h_ref[pl.ds(c, 1), :]), _staged_matrix(stages[slot]))

    def finish(c, slot, a8):
        a = a8[0:1] + a8[1:2]
        act = 0.5 * a * (1.0 + lax.erf(a * np.float32(1.0 / np.sqrt(2.0))))
        w_ref[pl.ds(c, 1), :] = gate_ref[pl.ds(c, 1), :] * act

    _token_pipeline(h_ref.shape[0], idx_ref, ut_vmem, stages, issue, finish,
                    jnp.zeros((8, N_SEL), jnp.float32))


def _peer_v_kernel(idx_ref, w_ref, x_ref, vt_hbm, out_ref, vt_vmem, *rest):
    stages, sem = rest[:-1], rest[-1]
    _load_table(vt_hbm, vt_vmem, sem)

    def issue(c, slot):
        w_col = jnp.broadcast_to(w_ref[pl.ds(c, 1), :], (N_SEL, N_SEL)).T
        los, his = [], []
        for s in range(ROW_PLANES):
            p = stages[slot][pl.ds(PLANE_ROWS * s, N_SEL), :]
            lo = pltpu.bitcast(p << 16, jnp.float32)
            hi = pltpu.bitcast(p & jnp.int32(-65536), jnp.float32)
            los.append(jnp.sum(lo * w_col, axis=0, keepdims=True))
            his.append(jnp.sum(hi * w_col, axis=0, keepdims=True))
        return jnp.concatenate(los + his, axis=1)

    def finish(c, slot, peer):
        out_ref[pl.ds(c, 1), :] = x_ref[pl.ds(c, 1), :] + peer

    _token_pipeline(x_ref.shape[0], idx_ref, vt_vmem, stages, issue, finish,
                    jnp.zeros((1, D_MODEL), jnp.float32))


def _peer_call(kernel_fn, name, idx, a, b, table, out_cols, group, tm=512):
    n = idx.shape[0]
    row = lambda i: (i, 0)
    return pl.pallas_call(
        kernel_fn,
        grid=(n // tm,),
        in_specs=[
            pl.BlockSpec((tm, N_SEL), row, memory_space=pltpu.SMEM),
            pl.BlockSpec((tm, a.shape[1]), row),
            pl.BlockSpec((tm, b.shape[1]), row),
            pl.BlockSpec(memory_space=pl.ANY),
        ],
        out_specs=pl.BlockSpec((tm, out_cols), row),
        out_shape=jax.ShapeDtypeStruct((n, out_cols), jnp.float32),
        scratch_shapes=[pltpu.VMEM((N_EXPERTS * ROW_PLANES, LANES), jnp.int32)]
        + [pltpu.VMEM((ROW_PLANES * PLANE_ROWS, LANES), jnp.int32)] * group
        + [pltpu.SemaphoreType.DMA],
        compiler_params=_params(),
        name=name,
    )(idx, a, b, table)


def _pack_table(t):
    tb = t.astype(jnp.bfloat16)
    lo = lax.bitcast_convert_type(tb[:, :ROW_WORDS], jnp.uint16).astype(jnp.uint32)
    hi = lax.bitcast_convert_type(tb[:, ROW_WORDS:], jnp.uint16).astype(jnp.uint32)
    words = lax.bitcast_convert_type(lo | (hi << 16), jnp.int32)
    return words.reshape(t.shape[0] * ROW_PLANES, LANES)


def _layer(x, p):
    b, t, _ = x.shape
    n = b * t
    xf = x.reshape(n, D_MODEL)
    u, q, k, v = _in_proj(xf, p["g_mix"], p["w_in"], p["hsum"], p["qg"], p["kg"])
    seq = lambda z: z.reshape(b, t, z.shape[-1])
    yc = _conv(seq(u), p["conv_w"], p["conv_b"], p["conv_ln_g"], p["conv_ln_b"], p["g_out_conv"])
    bias = _natten_bias(p["rpb"], t // GRID_W)
    ya = _natten(seq(q), seq(k), seq(v), bias, p["g_out_attn"])
    x1, h2, idx, gate = _out_proj(xf, yc.reshape(n, D_CONV), ya.reshape(n, D_ATTN), p["w_out"],
                                  p["g_ffn"], p["w_query"], p["sub_keys"])
    w = _peer_call(_peer_u_kernel, "peer_u", idx, h2, gate, p["ut"], N_SEL, group=8)
    y = _peer_call(_peer_v_kernel, "peer_v", idx, w, x1, p["vt"], D_MODEL, group=8)
    return y.reshape(b, t, D_MODEL)


def kernel(x_prompt, x_sample, g_mix, w_in, conv_w, conv_b, conv_ln_g, conv_ln_b, q_norm_g, k_norm_g, rpb, g_out_conv, g_out_attn, w_out, g_ffn, w_query, sub_keys, expert_u, expert_v):
    depth = g_mix.shape[0]
    head = np.arange(D_ATTN) // HEAD_DIM
    hsum = jnp.asarray((head[:, None] == head[None, :]) / HEAD_DIM, jnp.bfloat16)
    scale = HEAD_DIM ** -0.5
    y_prompt, y_sample = x_prompt, x_sample
    for l in range(depth):
        row = lambda z: z[l].reshape(1, -1)
        p = dict(
            g_mix=row(g_mix), w_in=w_in[l].astype(jnp.bfloat16), hsum=hsum,
            qg=jnp.tile(q_norm_g[l], N_HEADS).reshape(1, -1) * scale,
            kg=jnp.tile(k_norm_g[l], N_HEADS).reshape(1, -1),
            conv_w=conv_w[l], conv_b=row(conv_b), conv_ln_g=row(conv_ln_g), conv_ln_b=row(conv_ln_b),
            g_out_conv=row(g_out_conv), g_out_attn=row(g_out_attn), rpb=rpb[l],
            w_out=w_out[l].astype(jnp.bfloat16), g_ffn=row(g_ffn),
            w_query=w_query[l].astype(jnp.bfloat16),
            sub_keys=sub_keys[l].reshape(2 * PEER_HEADS, PEER_KEYS, PEER_DK_HALF).astype(jnp.bfloat16),
            ut=_pack_table(expert_u[l]), vt=_pack_table(expert_v[l]),
        )
        y_prompt = _layer(y_prompt, p)
        y_sample = _layer(y_sample, p)
    return (y_prompt, y_sample)
```
